```python
import math
import jax, jax.numpy as jnp
from jax import lax
import numpy as np

D_MODEL = 2048
BATCH = 1
SEQ = 8192
DEPTH = 2
DEC_BATCH = 128
DEC_SEQ = 4
PAST_LEN = 8192
PAGE_SIZE = 128

HEAD_DIM = 64
ROPE_THETA = 10000.0
EPS = 1e-6
WINDOW = 128
H_A = 64
KV_A = 8
G_A = H_A // KV_A
W_A = H_A * HEAD_DIM
VD_B = 2 * HEAD_DIM
H_B = (2 * D_MODEL) // VD_B
KV_B = 4
G_B = H_B // KV_B
W_B = H_B * VD_B
Q_BLOCK = 128
N_A_LAYERS = (DEPTH + 1) // 2
N_B_LAYERS = DEPTH // 2
NEG = -1e30

kernel_name = "hybrid_swa_sink_diffattn_step"


def _rmsnorm(x, g):
    xf = x.astype(jnp.float32)
    y = xf * lax.rsqrt(jnp.mean(xf * xf, axis=-1, keepdims=True) + EPS)
    return (y * g.astype(jnp.float32)).astype(x.dtype)


def _rope(x, pos):
    half = HEAD_DIM // 2
    inv_freq = 1.0 / (ROPE_THETA ** (jnp.arange(0, HEAD_DIM, 2, dtype=jnp.float32) / HEAD_DIM))
    ang = pos.astype(jnp.float32)[:, None] * inv_freq[None, :]
    shp = (pos.shape[0],) + (1,) * (x.ndim - 3) + (half,)
    cos = jnp.cos(ang).reshape(shp)
    sin = jnp.sin(ang).reshape(shp)
    xf = x.astype(jnp.float32)
    x1, x2 = xf[..., :half], xf[..., half:]
    return jnp.concatenate([x1 * cos - x2 * sin, x2 * cos + x1 * sin], axis=-1).astype(x.dtype)


def _gated_out(o, z, w_out):
    return (o * jax.nn.silu(z)) @ w_out


def _proj_a(x, g, w_in, qn, kn, pos):
    b, s, _ = x.shape
    h = _rmsnorm(x, g) @ w_in
    q, k, v, z = jnp.split(h, [W_A, W_A + KV_A * HEAD_DIM, W_A + 2 * KV_A * HEAD_DIM], axis=-1)
    q = _rope(_rmsnorm(q.reshape(b, s, KV_A, G_A, HEAD_DIM), qn), pos)
    k = _rope(_rmsnorm(k.reshape(b, s, KV_A, HEAD_DIM), kn), pos)
    v = v.reshape(b, s, KV_A, HEAD_DIM)
    return q, k, v, z


def _sink_core(q, k, v, q_pos, k_pos, sinks):
    s = jnp.einsum('bnqkgd,bnskd->bnkgqs', q, k, preferred_element_type=jnp.float32) * (HEAD_DIM ** -0.5)
    d = q_pos[:, :, None] - k_pos[:, None, :]
    valid = (d >= 0) & (d < WINDOW) & (k_pos[:, None, :] >= 0)
    s = jnp.where(valid[None, :, None, None], s, NEG)
    sink = sinks.astype(jnp.float32).reshape(1, 1, KV_A, G_A, 1, 1)
    m = jnp.maximum(jnp.max(s, axis=-1, keepdims=True), sink)
    p = jnp.exp(s - m)
    p = p / (jnp.sum(p, axis=-1, keepdims=True) + jnp.exp(sink - m))
    o = jnp.einsum('bnkgqs,bnskd->bnqkgd', p, v.astype(jnp.float32))
    return o.astype(q.dtype)


def _window_prompt(q, k, v, sinks):
    b, s = q.shape[:2]
    nb = s // WINDOW
    qb = q.reshape(b, nb, WINDOW, KV_A, G_A, HEAD_DIM)
    kb = k.reshape(b, nb, WINDOW, KV_A, HEAD_DIM)
    vb = v.reshape(b, nb, WINDOW, KV_A, HEAD_DIM)
    kk = jnp.concatenate([jnp.concatenate([jnp.zeros_like(kb[:, :1]), kb[:, :-1]], axis=1), kb], axis=2)
    vv = jnp.concatenate([jnp.concatenate([jnp.zeros_like(vb[:, :1]), vb[:, :-1]], axis=1), vb], axis=2)
    q_pos = jnp.arange(s, dtype=jnp.int32).reshape(nb, WINDOW)
    k_pos = jnp.concatenate([q_pos - WINDOW, q_pos], axis=1)
    o = _sink_core(qb, kk, vv, q_pos, k_pos, sinks)
    return o.reshape(b, s, W_A)


def _window_sample(q, k, v, buf_k, buf_v, past_len, sinks):
    b, ds = q.shape[:2]
    buf = buf_k.shape[1]
    kk = jnp.concatenate([buf_k, k], axis=1)
    vv = jnp.concatenate([buf_v, v], axis=1)
    k_pos = past_len - buf + jnp.arange(buf + ds, dtype=jnp.int32)
    q_pos = past_len + jnp.arange(ds, dtype=jnp.int32)
    o = _sink_core(q[:, None], kk[:, None], vv[:, None], q_pos[None], k_pos[None], sinks)
    return o.reshape(b, ds, W_A), kk[:, ds:], vv[:, ds:]


def _proj_b(x, g, w_in, qn, kn, pos):
    b, s, _ = x.shape
    h = _rmsnorm(x, g) @ w_in
    kw = KV_B * 2 * HEAD_DIM
    q, k, v, z = jnp.split(h, [H_B * 2 * HEAD_DIM, H_B * 2 * HEAD_DIM + kw, H_B * 2 * HEAD_DIM + 2 * kw], axis=-1)
    q = _rope(_rmsnorm(q.reshape(b, s, KV_B, G_B, 2, HEAD_DIM), qn), pos)
    k = _rope(_rmsnorm(k.reshape(b, s, KV_B, 2, HEAD_DIM), kn), pos)
    v = v.reshape(b, s, KV_B, VD_B)
    return q, k, v, z


def _diff_core(q, k, v, q_pos, k_pos, lam):
    s = jnp.einsum('bqkgmd,bskmd->bkgmqs', q, k, preferred_element_type=jnp.float32) * (HEAD_DIM ** -0.5)
    mask = k_pos[None, :] <= q_pos[:, None]
    p = jax.nn.softmax(jnp.where(mask, s, NEG), axis=-1)
    a = p[:, :, :, 0] - lam * p[:, :, :, 1]
    o = jnp.einsum('bkgqs,bskd->bqkgd', a, v.astype(jnp.float32))
    return o.astype(q.dtype)


def _diff_prompt(q, k, v, pos, lam):
    b, s = q.shape[:2]
    nb = s // Q_BLOCK
    qb = jnp.swapaxes(q.reshape(b, nb, Q_BLOCK, KV_B, G_B, 2, HEAD_DIM), 0, 1)
    pb = pos.reshape(nb, Q_BLOCK)
    o = lax.map(lambda a: _diff_core(a[0], k, v, a[1], pos, lam), (qb, pb))
    return jnp.swapaxes(o, 0, 1).reshape(b, s, KV_B, G_B, VD_B)


def _diff_sample(q, k, v, pool_k, pool_v, page_table, lam):
    ds = q.shape[1]
    past_len = page_table.shape[1] * pool_k.shape[1]
    q_pos = past_len + jnp.arange(ds, dtype=jnp.int32)
    k_pos = jnp.arange(past_len + ds, dtype=jnp.int32)

    def one(a):
        pt, qi, ki, vi = a
        kk = jnp.concatenate([pool_k[pt].reshape(past_len, KV_B, 2, HEAD_DIM), ki], axis=0)
        vv = jnp.concatenate([pool_v[pt].reshape(past_len, KV_B, VD_B), vi], axis=0)
        return _diff_core(qi[None], kk[None], vv[None], q_pos, k_pos, lam)[0]

    return lax.map(one, (page_table, q, k, v))


def setup_inputs(seed: int = 0) -> dict:
    key = jax.random.key(seed)
    ks = jax.random.split(key, 24)
    f32 = jnp.float32
    n_pages = PAST_LEN // PAGE_SIZE
    n_used = DEC_BATCH * n_pages
    n_pool = n_used + max(1, n_used // 4)
    win = min(WINDOW, PAST_LEN)
    in_a = W_A + 2 * KV_A * HEAD_DIM + W_A
    in_b = H_B * 2 * HEAD_DIM + 2 * KV_B * 2 * HEAD_DIM + W_B
    nrm = lambda k, shp: jax.random.normal(k, shp, f32)
    gain = lambda k, shp: 1.0 + 0.02 * nrm(k, shp)
    page_table = jax.random.permutation(ks[6], n_pool)[:n_used].reshape(DEC_BATCH, n_pages).astype(jnp.int32)
    return {
        "x_prompt": nrm(ks[0], (BATCH, SEQ, D_MODEL)),
        "x_sample": nrm(ks[1], (DEC_BATCH, DEC_SEQ, D_MODEL)),
        "cache_win_k": nrm(ks[2], (N_A_LAYERS, DEC_BATCH, win, KV_A, HEAD_DIM)),
        "cache_win_v": nrm(ks[3], (N_A_LAYERS, DEC_BATCH, win, KV_A, HEAD_DIM)),
        "cache_k": nrm(ks[4], (N_B_LAYERS, n_pool, PAGE_SIZE, KV_B, 2 * HEAD_DIM)),
        "cache_v": nrm(ks[5], (N_B_LAYERS, n_pool, PAGE_SIZE, KV_B, VD_B)),
        "page_table": page_table,
        "norm_a": gain(ks[7], (N_A_LAYERS, D_MODEL)),
        "w_in_a": nrm(ks[8], (N_A_LAYERS, D_MODEL, in_a)) * D_MODEL ** -0.5,
        "q_norm_a": gain(ks[9], (N_A_LAYERS, HEAD_DIM)),
        "k_norm_a": gain(ks[10], (N_A_LAYERS, HEAD_DIM)),
        "sinks_a": 0.5 * nrm(ks[11], (N_A_LAYERS, H_A)),
        "w_out_a": nrm(ks[12], (N_A_LAYERS, W_A, D_MODEL)) * W_A ** -0.5,
        "norm_b": gain(ks[13], (N_B_LAYERS, D_MODEL)),
        "w_in_b": nrm(ks[14], (N_B_LAYERS, D_MODEL, in_b)) * D_MODEL ** -0.5,
        "q_norm_b": gain(ks[15], (N_B_LAYERS, HEAD_DIM)),
        "k_norm_b": gain(ks[16], (N_B_LAYERS, HEAD_DIM)),
        "lambda_q1": 0.1 * nrm(ks[17], (N_B_LAYERS, HEAD_DIM)),
        "lambda_k1": 0.1 * nrm(ks[18], (N_B_LAYERS, HEAD_DIM)),
        "lambda_q2": 0.1 * nrm(ks[19], (N_B_LAYERS, HEAD_DIM)),
        "lambda_k2": 0.1 * nrm(ks[20], (N_B_LAYERS, HEAD_DIM)),
        "subln_b": gain(ks[21], (N_B_LAYERS, VD_B)),
        "w_out_b": nrm(ks[22], (N_B_LAYERS, W_B, D_MODEL)) * W_B ** -0.5,
    }


def reference(x_prompt, x_sample, cache_win_k, cache_win_v, cache_k, cache_v, page_table,
              norm_a, w_in_a, q_norm_a, k_norm_a, sinks_a, w_out_a,
              norm_b, w_in_b, q_norm_b, k_norm_b, lambda_q1, lambda_k1, lambda_q2, lambda_k2,
              subln_b, w_out_b):
    seq = x_prompt.shape[1]
    dec_seq = x_sample.shape[1]
    past_len = page_table.shape[1] * cache_k.shape[2]
    pos_p = jnp.arange(seq, dtype=jnp.int32)
    pos_s = past_len + jnp.arange(dec_seq, dtype=jnp.int32)
    xp, xs = x_prompt, x_sample
    wkp, wvp, wks, wvs = [], [], [], []
    pkp, pvp, pks, pvs = [], [], [], []
    for i in range(DEPTH):
        j = i // 2
        if i % 2 == 0:
            qp, kp, vp, zp = _proj_a(xp, norm_a[j], w_in_a[j], q_norm_a[j], k_norm_a[j], pos_p)
            qs, ks_, vs, zs = _proj_a(xs, norm_a[j], w_in_a[j], q_norm_a[j], k_norm_a[j], pos_s)
            op = _window_prompt(qp, kp, vp, sinks_a[j])
            os_, nbk, nbv = _window_sample(qs, ks_, vs, cache_win_k[j], cache_win_v[j], past_len, sinks_a[j])
            xp = xp + _gated_out(op, zp, w_out_a[j])
            xs = xs + _gated_out(os_, zs, w_out_a[j])
            wl = min(WINDOW, seq)
            wkp.append(kp[:, seq - wl:])
            wvp.append(vp[:, seq - wl:])
            wks.append(nbk)
            wvs.append(nbv)
        else:
            lam_init = 0.8 - 0.6 * math.exp(-0.3 * i)
            lam = (jnp.exp(jnp.sum(lambda_q1[j].astype(jnp.float32) * lambda_k1[j].astype(jnp.float32)))
                   - jnp.exp(jnp.sum(lambda_q2[j].astype(jnp.float32) * lambda_k2[j].astype(jnp.float32)))
                   + lam_init)
            qp, kp, vp, zp = _proj_b(xp, norm_b[j], w_in_b[j], q_norm_b[j], k_norm_b[j], pos_p)
            qs, ks_, vs, zs = _proj_b(xs, norm_b[j], w_in_b[j], q_norm_b[j], k_norm_b[j], pos_s)
            op = _diff_prompt(qp, kp, vp, pos_p, lam)
            os_ = _diff_sample(qs, ks_, vs, cache_k[j], cache_v[j], page_table, lam)
            op = (_rmsnorm(op, subln_b[j]) * (1.0 - lam_init)).reshape(xp.shape[0], seq, W_B)
            os_ = (_rmsnorm(os_, subln_b[j]) * (1.0 - lam_init)).reshape(xs.shape[0], dec_seq, W_B)
            xp = xp + _gated_out(op, zp, w_out_b[j])
            xs = xs + _gated_out(os_, zs, w_out_b[j])
            bp = xp.shape[0]
            pkp.append(kp.reshape(bp, seq // PAGE_SIZE, PAGE_SIZE, KV_B, 2 * HEAD_DIM))
            pvp.append(vp.reshape(bp, seq // PAGE_SIZE, PAGE_SIZE, KV_B, VD_B))
            pks.append(ks_.reshape(xs.shape[0], dec_seq, KV_B, 2 * HEAD_DIM))
            pvs.append(vs)
    return (xp, xs, jnp.stack(wkp), jnp.stack(wvp), jnp.stack(wks), jnp.stack(wvs),
            jnp.stack(pkp), jnp.stack(pvp), jnp.stack(pks), jnp.stack(pvs))
```

```python
import functools
import math

import jax
import jax.numpy as jnp
from jax import lax
from jax.experimental import pallas as pl
from jax.experimental.pallas import tpu as pltpu

F32 = jnp.float32
BF16 = jnp.bfloat16

D_MODEL = 2048
HEAD_DIM = 64
ROPE_THETA = 10000.0
EPS = 1e-6
WINDOW = 128
KV_A = 8
G_A = 8
KV_B = 4
G_B = 8
VD_B = 128
PAGE = 128
NEG = -1e30

W_Q = 4096
W_K = 512
W_V = 512
W_Z = 4096
W_IN = W_Q + W_K + W_V + W_Z
COL_K = W_Q
COL_V = W_Q + W_K
COL_Z = W_Q + W_K + W_V

LANES = 128
VMEM_LIMIT = 56 * 1024 * 1024

TM = 512
TN_IN = 1024
TN_OUT = 512
TQ = 256
TK = 256
NP = 8

_NT = (((1,), (1,)), ((), ()))


def _cparams(n_axes):
    return pltpu.CompilerParams(
        dimension_semantics=("arbitrary",) * n_axes,
        vmem_limit_bytes=VMEM_LIMIT,
    )


def _silu(z):
    return z * (1.0 / (1.0 + jnp.exp(-z)))


def _inproj_body(xp_ref, xs_ref, g_ref, w_ref, h_ref, xn_ref, *, n_prompt_tiles):
    i = pl.program_id(0)
    j = pl.program_id(1)

    def norm(x_ref):
        x = x_ref[...]
        ms = jnp.mean(x * x, axis=-1, keepdims=True)
        xn_ref[...] = (x * lax.rsqrt(ms + EPS) * g_ref[...]).astype(BF16)

    @pl.when((j == 0) & (i < n_prompt_tiles))
    def _():
        norm(xp_ref)

    @pl.when((j == 0) & (i >= n_prompt_tiles))
    def _():
        norm(xs_ref)

    h_ref[...] = jnp.dot(xn_ref[...], w_ref[...], preferred_element_type=F32)


def _inproj(xp, xs, s_blk, gain, w_bf16, n_rows):
    n_prompt_tiles = (n_rows - TM) // TM
    grid = (n_rows // TM, W_IN // TN_IN)
    return pl.pallas_call(
        functools.partial(_inproj_body, n_prompt_tiles=n_prompt_tiles),
        grid=grid,
        in_specs=[
            pl.BlockSpec((TM, D_MODEL), lambda i, j: (jnp.minimum(i, n_prompt_tiles - 1), 0)),
            pl.BlockSpec((TM, D_MODEL), lambda i, j: (s_blk, 0)),
            pl.BlockSpec((1, D_MODEL), lambda i, j: (0, 0)),
            pl.BlockSpec((D_MODEL, TN_IN), lambda i, j: (0, j)),
        ],
        out_specs=pl.BlockSpec((TM, TN_IN), lambda i, j: (i, j)),
        out_shape=jax.ShapeDtypeStruct((n_rows, W_IN), F32),
        scratch_shapes=[pltpu.VMEM((TM, D_MODEL), BF16)],
        compiler_params=_cparams(2),
        name="inproj",
    )(xp, xs, gain.reshape(1, D_MODEL), w_bf16)


TR = 256
CW = 256


def _qkpost_body(h_ref, cos_ref, sin_ref, qg_ref, kg_ref, bd_ref, q_ref, k_ref):
    cos = jnp.concatenate([cos_ref[...]] * (CW // LANES), axis=1)
    sin = jnp.concatenate([sin_ref[...]] * (CW // LANES), axis=1)
    bd = bd_ref[...]
    lane = lax.broadcasted_iota(jnp.int32, (TR, CW), 1)
    first_half = (lane & (HEAD_DIM - 1)) < (HEAD_DIM // 2)

    def normrope(c, gain):
        t = h_ref[:, c * CW:(c + 1) * CW]
        ss = jnp.dot((t * t).astype(BF16), bd, preferred_element_type=F32)
        tn = t * lax.rsqrt(ss * (1.0 / HEAD_DIM) + EPS) * gain
        partner = jnp.where(first_half,
                            pltpu.roll(tn, CW - HEAD_DIM // 2, 1),
                            pltpu.roll(tn, HEAD_DIM // 2, 1))
        return tn * cos + partner * sin

    qg = qg_ref[...]
    kg = kg_ref[...]
    for c in range(W_Q // CW):
        q_ref[:, c * CW:(c + 1) * CW] = (normrope(c, qg) * (HEAD_DIM ** -0.5)).astype(BF16)
    for c in range(W_K // CW):
        k_ref[:, c * CW:(c + 1) * CW] = normrope(W_Q // CW + c, kg)


def _qkpost(h, cos, sin, q_gain, k_gain, n_rows):
    head = jnp.arange(CW, dtype=jnp.int32) // HEAD_DIM
    bd = (head[:, None] == head[None, :]).astype(BF16)
    qg = jnp.tile(q_gain.astype(F32), CW // HEAD_DIM).reshape(1, CW)
    kg = jnp.tile(k_gain.astype(F32), CW // HEAD_DIM).reshape(1, CW)
    return pl.pallas_call(
        _qkpost_body,
        grid=(n_rows // TR,),
        in_specs=[
            pl.BlockSpec((TR, W_Q + W_K), lambda i: (i, 0)),
            pl.BlockSpec((TR, LANES), lambda i: (i, 0)),
            pl.BlockSpec((TR, LANES), lambda i: (i, 0)),
            pl.BlockSpec((1, CW), lambda i: (0, 0)),
            pl.BlockSpec((1, CW), lambda i: (0, 0)),
            pl.BlockSpec((CW, CW), lambda i: (0, 0)),
        ],
        out_specs=[
            pl.BlockSpec((TR, W_Q), lambda i: (i, 0)),
            pl.BlockSpec((TR, W_K), lambda i: (i, 0)),
        ],
        out_shape=[
            jax.ShapeDtypeStruct((n_rows, W_Q), BF16),
            jax.ShapeDtypeStruct((n_rows, W_K), F32),
        ],
        compiler_params=_cparams(1),
        name="qkpost",
    )(h, cos, sin, qg, kg, bd)


def _half_split(x, lo_half, even):
    if even:
        a = jnp.where(lo_half, x, 0.0)
        return a, pltpu.roll(a, HEAD_DIM, 1)
    b = jnp.where(lo_half, 0.0, x)
    return pltpu.roll(b, HEAD_DIM, 1), b


def _win_prompt_body(sink_ref, q_ref, kp_ref, kc_ref, vp_ref, vc_ref,
                     z0_ref, z1_ref, z2_ref, z3_ref, o_ref):
    n = pl.program_id(0)
    z_refs = (z0_ref, z1_ref, z2_ref, z3_ref)
    nk = 2 * WINDOW
    lo_half = lax.broadcasted_iota(jnp.int32, (nk, LANES), 1) < HEAD_DIM
    rows = 4 * WINDOW
    qi = lax.broadcasted_iota(jnp.int32, (rows, nk), 0) & (WINDOW - 1)
    ki = lax.broadcasted_iota(jnp.int32, (rows, nk), 1)
    valid = (ki > qi) & (ki <= qi + WINDOW) & (ki >= jnp.where(n > 0, 0, WINDOW))

    def softmax(s, sink):
        s = jnp.where(valid, s, NEG)
        m = jnp.maximum(jnp.max(s, axis=-1, keepdims=True), sink)
        p = jnp.exp(s - m)
        l = jnp.sum(p, axis=-1, keepdims=True) + jnp.exp(sink - m)
        return p.astype(BF16), 1.0 / l

    for pr in range(KV_A // 2):
        cs = slice(pr * LANES, (pr + 1) * LANES)
        kk = jnp.concatenate([kp_ref[:, cs], kc_ref[:, cs]], axis=0)
        vv = jnp.concatenate([vp_ref[:, cs], vc_ref[:, cs]], axis=0)
        for sub in range(2):
            kv = 2 * pr + sub
            ka, kb = _half_split(kk, lo_half, sub == 0)
            va, vb = _half_split(vv, lo_half, sub == 0)
            base = kv * G_A * HEAD_DIM
            q4 = jnp.concatenate(
                [q_ref[:, base + j * LANES: base + (j + 1) * LANES] for j in range(4)], axis=0)
            s_e = lax.dot_general(q4, ka.astype(BF16), _NT, preferred_element_type=F32)
            s_o = lax.dot_general(q4, kb.astype(BF16), _NT, preferred_element_type=F32)
            sink_e = jnp.concatenate(
                [jnp.full((WINDOW, 1), sink_ref[kv * G_A + 2 * j], F32) for j in range(4)], axis=0)
            sink_o = jnp.concatenate(
                [jnp.full((WINDOW, 1), sink_ref[kv * G_A + 2 * j + 1], F32) for j in range(4)], axis=0)
            p_e, r_e = softmax(s_e, sink_e)
            p_o, r_o = softmax(s_o, sink_o)
            o = (jnp.dot(p_e, va.astype(BF16), preferred_element_type=F32) * r_e
                 + jnp.dot(p_o, vb.astype(BF16), preferred_element_type=F32) * r_o)
            zr = z_refs[kv // 2]
            zb = (kv % 2) * 512
            for j in range(4):
                z = zr[:, zb + j * LANES: zb + (j + 1) * LANES]
                o_ref[:, base + j * LANES: base + (j + 1) * LANES] = (
                    o[j * WINDOW:(j + 1) * WINDOW] * _silu(z)).astype(BF16)


def _win_prompt(q, k, h, sinks, seq):
    nb = seq // WINDOW
    prev = lambda n: (jnp.maximum(n - 1, 0), 0)
    zspec = lambda c: pl.BlockSpec((WINDOW, 1024), lambda n: (n, COL_Z // 1024 + c))
    return pl.pallas_call(
        _win_prompt_body,
        grid=(nb,),
        in_specs=[
            pl.BlockSpec(memory_space=pltpu.SMEM),
            pl.BlockSpec((WINDOW, W_Q), lambda n: (n, 0)),
            pl.BlockSpec((WINDOW, W_K), prev),
            pl.BlockSpec((WINDOW, W_K), lambda n: (n, 0)),
            pl.BlockSpec((WINDOW, W_V), lambda n: (jnp.maximum(n - 1, 0), COL_V // W_V)),
            pl.BlockSpec((WINDOW, W_V), lambda n: (n, COL_V // W_V)),
            zspec(0), zspec(1), zspec(2), zspec(3),
        ],
        out_specs=pl.BlockSpec((WINDOW, W_Q), lambda n: (n, 0)),
        out_shape=jax.ShapeDtypeStruct((seq, W_Q), BF16),
        compiler_params=_cparams(1),
        name="win_prompt",
    )(sinks.astype(F32), q, k, k, h, h, h, h, h, h)


def _win_sample_body(qd_ref, sink_ref, ck_ref, cv_ref, kn_ref, vn_ref,
                     o_ref, ok_ref, ov_ref, kall_ref, vall_ref, *, dec_seq):
    keep = WINDOW - dec_seq
    ok_ref[0:keep, :] = ck_ref[dec_seq:WINDOW, :]
    ok_ref[keep:WINDOW, :] = kn_ref[...]
    ov_ref[0:keep, :] = cv_ref[dec_seq:WINDOW, :]
    ov_ref[keep:WINDOW, :] = vn_ref[...]

    nk = 2 * WINDOW
    for src, new, dst in ((ck_ref, kn_ref, kall_ref), (cv_ref, vn_ref, vall_ref)):
        dst[0:WINDOW, :] = src[...]
        dst[WINDOW:nk, :] = jnp.zeros((WINDOW, KV_A * HEAD_DIM), F32)
        dst[WINDOW:WINDOW + dec_seq, :] = new[...]
    kall = kall_ref[...]
    vall = vall_ref[...]
    rows = dec_seq * G_A
    t = lax.broadcasted_iota(jnp.int32, (rows, nk), 0) >> (G_A.bit_length() - 1)
    i = lax.broadcasted_iota(jnp.int32, (rows, nk), 1)
    valid = (i > t) & (i <= WINDOW + t)
    lo_k = lax.broadcasted_iota(jnp.int32, (nk, LANES), 1) < HEAD_DIM
    lo_o = lax.broadcasted_iota(jnp.int32, (rows, LANES), 1) < HEAD_DIM

    for pr in range(KV_A // 2):
        cs = slice(pr * LANES, (pr + 1) * LANES)
        kk = kall[:, cs]
        vvb = vall[:, cs].astype(BF16)
        outs = []
        for sub in range(2):
            kv = 2 * pr + sub
            km = jnp.where(lo_k if sub == 0 else jnp.logical_not(lo_k), kk, 0.0).astype(BF16)
            s = lax.dot_general(qd_ref[kv], km, _NT, preferred_element_type=F32)
            s = jnp.where(valid, s, NEG)
            sink = sink_ref[kv]
            m = jnp.maximum(jnp.max(s, axis=-1, keepdims=True), sink)
            p = jnp.exp(s - m)
            l = jnp.sum(p, axis=-1, keepdims=True) + jnp.exp(sink - m)
            outs.append(jnp.dot(p.astype(BF16), vvb, preferred_element_type=F32) * (1.0 / l))
        o_ref[pr] = jnp.where(lo_o, outs[0], outs[1])


def _win_sample(qd, sink_tab, ck, cv, kn, vn, dec_batch, dec_seq):
    rows = dec_seq * G_A
    wkv = KV_A * HEAD_DIM
    bspec = lambda shp: pl.BlockSpec((None,) + shp, lambda b: (b,) + (0,) * len(shp))
    return pl.pallas_call(
        functools.partial(_win_sample_body, dec_seq=dec_seq),
        grid=(dec_batch,),
        in_specs=[
            bspec((KV_A, rows, LANES)),
            pl.BlockSpec((KV_A, rows, 1), lambda b: (0, 0, 0)),
            bspec((WINDOW, wkv)), bspec((WINDOW, wkv)),
            bspec((dec_seq, wkv)), bspec((dec_seq, wkv)),
        ],
        out_specs=[
            bspec((KV_A // 2, rows, LANES)),
            bspec((WINDOW, wkv)), bspec((WINDOW, wkv)),
        ],
        out_shape=[
            jax.ShapeDtypeStruct((dec_batch, KV_A // 2, rows, LANES), F32),
            jax.ShapeDtypeStruct((dec_batch, WINDOW, wkv), F32),
            jax.ShapeDtypeStruct((dec_batch, WINDOW, wkv), F32),
        ],
        scratch_shapes=[pltpu.VMEM((2 * WINDOW, wkv), F32), pltpu.VMEM((2 * WINDOW, wkv), F32)],
        compiler_params=_cparams(1),
        name="win_sample",
    )(qd, sink_tab, ck, cv, kn, vn)


def _lambda(lq1_ref, lk1_ref, lq2_ref, lk2_ref, lam_init):
    a = jnp.sum(lq1_ref[...] * lk1_ref[...], axis=-1, keepdims=True)
    b = jnp.sum(lq2_ref[...] * lk2_ref[...], axis=-1, keepdims=True)
    return jnp.exp(a) - jnp.exp(b) + lam_init


def _subln(o, gain, lam_init):
    ms = jnp.mean(o * o, axis=-1, keepdims=True)
    return o * lax.rsqrt(ms + EPS) * gain * (1.0 - lam_init)


def _rep(x, n):
    return jnp.concatenate([x] * n, axis=1) if n > 1 else x


def _diff_prompt_body(lq1_ref, lk1_ref, lq2_ref, lk2_ref, sub_ref,
                      q_ref, k_ref, v_ref, z_ref, o_ref,
                      q8_ref, m_ref, l_ref, acc_ref, *, lam_init):
    qt = pl.program_id(1)
    rows = G_B * TQ
    for g in range(G_B):
        q8_ref[g * TQ:(g + 1) * TQ, :] = q_ref[:, g * LANES:(g + 1) * LANES]
    m_ref[...] = jnp.full(m_ref.shape, NEG, F32)
    l_ref[...] = jnp.zeros(l_ref.shape, F32)
    acc_ref[...] = jnp.zeros(acc_ref.shape, F32)

    lo_half = lax.broadcasted_iota(jnp.int32, (TK, LANES), 1) < HEAD_DIM

    def tile(kt, masked):
        koff = pl.multiple_of(kt * TK, TK)
        kf = k_ref[pl.ds(koff, TK), :]
        vb = v_ref[pl.ds(koff, TK), :].astype(BF16)
        kxs = (jnp.where(lo_half, kf, 0.0).astype(BF16),
               jnp.where(lo_half, 0.0, kf).astype(BF16))
        q8 = q8_ref[...]
        if masked:
            qi = lax.broadcasted_iota(jnp.int32, (rows, TK), 0) & (TQ - 1)
            ki = lax.broadcasted_iota(jnp.int32, (rows, TK), 1)
            causal = ki <= qi
        for mi in range(2):
            s = lax.dot_general(q8, kxs[mi], _NT, preferred_element_type=F32)
            if masked:
                s = jnp.where(causal, s, NEG)
            m_old = m_ref[mi]
            m_new = jnp.maximum(m_old, jnp.max(s, axis=-1, keepdims=True))
            alpha = jnp.exp(m_old - m_new)
            p = jnp.exp(s - _rep(m_new, TK // LANES))
            l_ref[mi] = alpha * l_ref[mi] + jnp.sum(p, axis=-1, keepdims=True)
            acc_ref[mi] = alpha * acc_ref[mi] + jnp.dot(
                p.astype(BF16), vb, preferred_element_type=F32)
            m_ref[mi] = m_new

    def body(kt, carry):
        tile(kt, False)
        return carry

    lax.fori_loop(0, qt, body, 0)
    tile(qt, True)

    lam = _lambda(lq1_ref, lk1_ref, lq2_ref, lk2_ref, lam_init)
    o = acc_ref[0] * (1.0 / l_ref[0]) - lam * (acc_ref[1] * (1.0 / l_ref[1]))
    on = _subln(o, sub_ref[...], lam_init)
    for g in range(G_B):
        z = z_ref[:, g * LANES:(g + 1) * LANES]
        o_ref[:, g * LANES:(g + 1) * LANES] = (on[g * TQ:(g + 1) * TQ] * _silu(z)).astype(BF16)


def _diff_prompt(q, k, h, lam_vecs, subln, lam_init, seq):
    gw = G_B * VD_B
    vec = pl.BlockSpec((1, HEAD_DIM), lambda c, t: (0, 0))
    return pl.pallas_call(
        functools.partial(_diff_prompt_body, lam_init=lam_init),
        grid=(KV_B, seq // TQ),
        in_specs=[
            vec, vec, vec, vec,
            pl.BlockSpec((1, VD_B), lambda c, t: (0, 0)),
            pl.BlockSpec((TQ, gw), lambda c, t: (t, c)),
            pl.BlockSpec((seq, VD_B), lambda c, t: (0, c)),
            pl.BlockSpec((seq, VD_B), lambda c, t: (0, COL_V // VD_B + c)),
            pl.BlockSpec((TQ, gw), lambda c, t: (t, COL_Z // gw + c)),
        ],
        out_specs=pl.BlockSpec((TQ, gw), lambda c, t: (t, c)),
        out_shape=jax.ShapeDtypeStruct((seq, W_Q), BF16),
        scratch_shapes=[
            pltpu.VMEM((G_B * TQ, LANES), BF16),
            pltpu.VMEM((2, G_B * TQ, LANES), F32),
            pltpu.VMEM((2, G_B * TQ, LANES), F32),
            pltpu.VMEM((2, G_B * TQ, VD_B), F32),
        ],
        compiler_params=_cparams(2),
        name="diff_prompt",
    )(*lam_vecs, subln.reshape(1, VD_B), q, k, h, h)


def _diff_sample_body(pt_ref, lq1_ref, lk1_ref, lq2_ref, lk2_ref, sub_ref,
                      qbd_ref, kn_ref, vn_ref, *rest, lam_init, dec_seq, n_chunks):
    kp_refs = rest[:NP]
    vp_refs = rest[NP:2 * NP]
    o_ref, m_ref, l_ref, acc_ref = rest[2 * NP:]
    c = pl.program_id(1)
    rows = dec_seq * G_B

    @pl.when(c == 0)
    def _():
        m_ref[...] = jnp.full(m_ref.shape, NEG, F32)
        l_ref[...] = jnp.zeros(l_ref.shape, F32)
        acc_ref[...] = jnp.zeros(acc_ref.shape, F32)

    def update(kv, s, vb):
        width = s.shape[1] // LANES
        m_old = m_ref[kv]
        m_new = jnp.maximum(m_old, jnp.max(s, axis=-1, keepdims=True))
        alpha = jnp.exp(m_old - m_new)
        p = jnp.exp(s - _rep(m_new, width))
        l_ref[kv] = alpha * l_ref[kv] + jnp.sum(p, axis=-1, keepdims=True)
        acc_ref[kv] = alpha * acc_ref[kv] + jnp.dot(p.astype(BF16), vb, preferred_element_type=F32)
        m_ref[kv] = m_new

    for kv in range(KV_B):
        rs = pl.ds(kv, PAGE, stride=KV_B)
        kb = jnp.concatenate([r[rs, :] for r in kp_refs], axis=0).astype(BF16)
        vb = jnp.concatenate([r[rs, :] for r in vp_refs], axis=0).astype(BF16)
        s = lax.dot_general(qbd_ref[kv], kb, _NT, preferred_element_type=F32)
        update(kv, s, vb)

    @pl.when(c == n_chunks - 1)
    def _():
        lam = _lambda(lq1_ref, lk1_ref, lq2_ref, lk2_ref, lam_init)
        r = lax.broadcasted_iota(jnp.int32, (2 * rows, LANES), 0)
        t = (r & (rows - 1)) >> (G_B.bit_length() - 1)
        i = lax.broadcasted_iota(jnp.int32, (2 * rows, LANES), 1)
        valid = i <= t
        zpad = jnp.zeros((LANES - 8, LANES), F32)
        for kv in range(KV_B):
            cs = slice(kv * LANES, (kv + 1) * LANES)
            kb = jnp.concatenate([kn_ref[:, cs], zpad], axis=0).astype(BF16)
            vb = jnp.concatenate([vn_ref[:, cs], zpad], axis=0).astype(BF16)
            s = lax.dot_general(qbd_ref[kv], kb, _NT, preferred_element_type=F32)
            update(kv, jnp.where(valid, s, NEG), vb)
            acc = acc_ref[kv]
            rl = 1.0 / l_ref[kv]
            o = acc[0:rows] * rl[0:rows] - lam * (acc[rows:2 * rows] * rl[rows:2 * rows])
            o_ref[kv] = _subln(o, sub_ref[...], lam_init)


def _diff_sample(pt_flat, qbd, kn8, vn8, cache_k, cache_v, lam_vecs, subln, lam_init,
                 dec_batch, dec_seq, n_pages):
    rows = dec_seq * G_B
    n_chunks = n_pages // NP
    wkv = KV_B * VD_B
    vec = pl.BlockSpec((1, HEAD_DIM), lambda b, c, pt: (0, 0))

    def page_spec(i):
        return pl.BlockSpec((None, PAGE * KV_B, VD_B),
                            lambda b, c, pt: (pt[b * n_pages + c * NP + i], 0, 0))

    grid_spec = pltpu.PrefetchScalarGridSpec(
        num_scalar_prefetch=1,
        grid=(dec_batch, n_chunks),
        in_specs=[
            vec, vec, vec, vec,
            pl.BlockSpec((1, VD_B), lambda b, c, pt: (0, 0)),
            pl.BlockSpec((None, KV_B, 2 * rows, LANES), lambda b, c, pt: (b, 0, 0, 0)),
            pl.BlockSpec((None, 8, wkv), lambda b, c, pt: (b, 0, 0)),
            pl.BlockSpec((None, 8, wkv), lambda b, c, pt: (b, 0, 0)),
        ] + [page_spec(i) for i in range(NP)] + [page_spec(i) for i in range(NP)],
        out_specs=pl.BlockSpec((None, KV_B, rows, VD_B), lambda b, c, pt: (b, 0, 0, 0)),
        scratch_shapes=[
            pltpu.VMEM((KV_B, 2 * rows, LANES), F32),
            pltpu.VMEM((KV_B, 2 * rows, LANES), F32),
            pltpu.VMEM((KV_B, 2 * rows, VD_B), F32),
        ],
    )
    return pl.pallas_call(
        functools.partial(_diff_sample_body, lam_init=lam_init, dec_seq=dec_seq,
                          n_chunks=n_chunks),
        grid_spec=grid_spec,
        out_shape=jax.ShapeDtypeStruct((dec_batch, KV_B, rows, VD_B), F32),
        compiler_params=_cparams(2),
        name="diff_sample",
    )(pt_flat, *lam_vecs, subln.reshape(1, VD_B), qbd, kn8, vn8,
      *([cache_k] * NP), *([cache_v] * NP))


def _gate_body(o_ref, z_ref, g_ref):
    g_ref[...] = (o_ref[...] * _silu(z_ref[...])).astype(BF16)


def _gate_sample(o_s, h, row_blk):
    n = o_s.shape[0]
    return pl.pallas_call(
        _gate_body,
        grid=(W_Z // 1024,),
        in_specs=[
            pl.BlockSpec((n, 1024), lambda c: (0, c)),
            pl.BlockSpec((n, 1024), lambda c: (row_blk, COL_Z // 1024 + c)),
        ],
        out_specs=pl.BlockSpec((n, 1024), lambda c: (0, c)),
        out_shape=jax.ShapeDtypeStruct((n, W_Z), BF16),
        compiler_params=_cparams(1),
        name="gate_sample",
    )(o_s, h)


def _outproj_body(gp_ref, gs_ref, w_ref, xp_ref, xs_ref, y_ref, *, n_prompt_tiles):
    i = pl.program_id(0)

    @pl.when(i < n_prompt_tiles)
    def _():
        y_ref[...] = xp_ref[...] + jnp.dot(gp_ref[...], w_ref[...], preferred_element_type=F32)

    @pl.when(i >= n_prompt_tiles)
    def _():
        y_ref[...] = xs_ref[...] + jnp.dot(gs_ref[...], w_ref[...], preferred_element_type=F32)


def _outproj(gp, gs, w_bf16, xp, xs, s_blk, n_rows):
    n_prompt_tiles = (n_rows - TM) // TM
    last = n_prompt_tiles - 1
    return pl.pallas_call(
        functools.partial(_outproj_body, n_prompt_tiles=n_prompt_tiles),
        grid=(n_rows // TM, D_MODEL // TN_OUT),
        in_specs=[
            pl.BlockSpec((TM, W_Z), lambda i, j: (jnp.minimum(i, last), 0)),
            pl.BlockSpec((TM, W_Z), lambda i, j: (0, 0)),
            pl.BlockSpec((W_Z, TN_OUT), lambda i, j: (0, j)),
            pl.BlockSpec((TM, TN_OUT), lambda i, j: (jnp.minimum(i, last), j)),
            pl.BlockSpec((TM, TN_OUT), lambda i, j: (s_blk, j)),
        ],
        out_specs=pl.BlockSpec((TM, TN_OUT), lambda i, j: (i, j)),
        out_shape=jax.ShapeDtypeStruct((n_rows, D_MODEL), F32),
        compiler_params=_cparams(2),
        name="outproj",
    )(gp, gs, w_bf16, xp, xs)


def _rope_tables(seq, dec_batch, dec_seq, past_len):
    half = HEAD_DIM // 2
    inv_freq = 1.0 / (ROPE_THETA ** (jnp.arange(0, HEAD_DIM, 2, dtype=F32) / HEAD_DIM))
    pos = jnp.concatenate([
        jnp.arange(seq, dtype=jnp.int32),
        jnp.tile(past_len + jnp.arange(dec_seq, dtype=jnp.int32), dec_batch),
    ])
    ang = pos.astype(F32)[:, None] * inv_freq[None, :]
    cos = jnp.cos(ang)
    sin = jnp.sin(ang)
    reps = LANES // HEAD_DIM
    cos_t = jnp.concatenate([cos, cos] * reps, axis=1)
    sin_t = jnp.concatenate([-sin, sin] * reps, axis=1)
    assert cos_t.shape[1] == LANES and half * 2 == HEAD_DIM
    return cos_t, sin_t


def kernel(x_prompt, x_sample, cache_win_k, cache_win_v, cache_k, cache_v, page_table,
           norm_a, w_in_a, q_norm_a, k_norm_a, sinks_a, w_out_a,
           norm_b, w_in_b, q_norm_b, k_norm_b, lambda_q1, lambda_k1, lambda_q2, lambda_k2,
           subln_b, w_out_b):
    batch, seq, _ = x_prompt.shape
    dec_batch, dec_seq, _ = x_sample.shape
    n_pool = cache_k.shape[1]
    n_pages = page_table.shape[1]
    past_len = n_pages * cache_k.shape[2]
    n_dec = dec_batch * dec_seq
    n_rows = seq + n_dec
    assert batch == 1 and n_dec == TM and seq % TM == 0 and cache_k.shape[2] == PAGE
    assert cache_win_k.shape[2] == WINDOW and n_pages % NP == 0
    s_blk = seq // TM

    cos_t, sin_t = _rope_tables(seq, dec_batch, dec_seq, past_len)
    xp = x_prompt.reshape(seq, D_MODEL)
    xs = x_sample.reshape(n_dec, D_MODEL)

    h = _inproj(xp, xs, 0, norm_a[0], w_in_a[0].astype(BF16), n_rows)
    q, k = _qkpost(h, cos_t, sin_t, q_norm_a[0], k_norm_a[0], n_rows)
    v = h[:, COL_V:COL_V + W_V]

    gp = _win_prompt(q, k, h, sinks_a[0], seq)

    rows_a = dec_seq * G_A
    qs = q[seq:].reshape(dec_batch, dec_seq, KV_A, G_A, HEAD_DIM)
    qs = qs.transpose(0, 2, 1, 3, 4).reshape(dec_batch, KV_A, rows_a, HEAD_DIM)
    qd = jnp.concatenate([qs, qs], axis=-1)
    sink_tab = jnp.broadcast_to(sinks_a[0].astype(F32).reshape(KV_A, 1, G_A),
                                (KV_A, dec_seq, G_A)).reshape(KV_A, rows_a, 1)
    wkv_a = KV_A * HEAD_DIM
    kn = k[seq:].reshape(dec_batch, dec_seq, wkv_a)
    vn = v[seq:].reshape(dec_batch, dec_seq, wkv_a)
    o_s, win_k_s, win_v_s = _win_sample(
        qd, sink_tab,
        cache_win_k[0].reshape(dec_batch, WINDOW, wkv_a),
        cache_win_v[0].reshape(dec_batch, WINDOW, wkv_a),
        kn, vn, dec_batch, dec_seq)
    o_s = o_s.reshape(dec_batch, KV_A // 2, dec_seq, G_A, 2, HEAD_DIM)
    o_s = o_s.transpose(0, 2, 1, 4, 3, 5).reshape(n_dec, W_Q)
    gs = _gate_sample(o_s, h, s_blk)

    x1 = _outproj(gp, gs, w_out_a[0].astype(BF16), xp, xs, 0, n_rows)

    wl = min(WINDOW, seq)
    win_k_p = k[seq - wl:seq].reshape(1, batch, wl, KV_A, HEAD_DIM)
    win_v_p = v[seq - wl:seq].reshape(1, batch, wl, KV_A, HEAD_DIM)
    win_k_s = win_k_s.reshape(1, dec_batch, WINDOW, KV_A, HEAD_DIM)
    win_v_s = win_v_s.reshape(1, dec_batch, WINDOW, KV_A, HEAD_DIM)

    lam_init = 0.8 - 0.6 * math.exp(-0.3 * 1)
    lam_vecs = [a[0].astype(F32).reshape(1, HEAD_DIM)
                for a in (lambda_q1, lambda_k1, lambda_q2, lambda_k2)]

    h = _inproj(x1, x1, s_blk, norm_b[0], w_in_b[0].astype(BF16), n_rows)
    q, k = _qkpost(h, cos_t, sin_t, q_norm_b[0], k_norm_b[0], n_rows)
    v = h[:, COL_V:COL_V + W_V]

    gp = _diff_prompt(q, k, h, lam_vecs, subln_b[0].astype(F32), lam_init, seq)

    rows_b = dec_seq * G_B
    qs = q[seq:].reshape(dec_batch, dec_seq, KV_B, G_B, 2, HEAD_DIM)
    qs = qs.transpose(0, 2, 4, 1, 3, 5).reshape(dec_batch, KV_B, 2, rows_b, HEAD_DIM)
    zq = jnp.zeros_like(qs[:, :, 0])
    qbd = jnp.concatenate([
        jnp.concatenate([qs[:, :, 0], zq], axis=-1),
        jnp.concatenate([zq, qs[:, :, 1]], axis=-1),
    ], axis=2)
    wkv_b = KV_B * VD_B
    kn = k[seq:].reshape(dec_batch, dec_seq, wkv_b)
    vn = v[seq:].reshape(dec_batch, dec_seq, wkv_b)
    padn = jnp.zeros((dec_batch, 8 - dec_seq, wkv_b), F32)
    kn8 = jnp.concatenate([kn, padn], axis=1)
    vn8 = jnp.concatenate([vn, padn], axis=1)
    o_s = _diff_sample(
        page_table.reshape(-1).astype(jnp.int32), qbd, kn8, vn8,
        cache_k[0].reshape(n_pool, PAGE * KV_B, VD_B), cache_v[0].reshape(n_pool, PAGE * KV_B, VD_B),
        lam_vecs, subln_b[0].astype(F32), lam_init, dec_batch, dec_seq, n_pages)
    o_s = o_s.reshape(dec_batch, KV_B, dec_seq, G_B * VD_B)
    o_s = o_s.transpose(0, 2, 1, 3).reshape(n_dec, W_Q)
    gs = _gate_sample(o_s, h, s_blk)

    x2 = _outproj(gp, gs, w_out_b[0].astype(BF16), x1, x1, s_blk, n_rows)

    y_prompt = x2[:seq].reshape(batch, seq, D_MODEL)
    y_sample = x2[seq:].reshape(dec_batch, dec_seq, D_MODEL)
    kv_k_p = k[:seq].reshape(1, batch, seq // PAGE, PAGE, KV_B, 2 * HEAD_DIM)
    kv_v_p = v[:seq].reshape(1, batch, seq // PAGE, PAGE, KV_B, VD_B)
    kv_k_s = kn.reshape(1, dec_batch, dec_seq, KV_B, 2 * HEAD_DIM)
    kv_v_s = vn.reshape(1, dec_batch, dec_seq, KV_B, VD_B)

    return (y_prompt, y_sample, win_k_p, win_v_p, win_k_s, win_v_s,
            kv_k_p, kv_v_p, kv_k_s, kv_v_s)
```

```python
import functools
import math

import jax
import jax.numpy as jnp
from jax import lax
from jax.experimental import pallas as pl
from jax.experimental.pallas import tpu as pltpu

F32 = jnp.float32
BF16 = jnp.bfloat16

D_MODEL = 2048
HEAD_DIM = 64
ROPE_THETA = 10000.0
EPS = 1e-6
WINDOW = 128
KV_A = 8
G_A = 8
KV_B = 4
G_B = 8
VD_B = 128
PAGE = 128
NEG = -1e30

W_Q = 4096
W_K = 512
W_V = 512
W_Z = 4096
W_IN = W_Q + W_K + W_V + W_Z
COL_K = W_Q
COL_V = W_Q + W_K
COL_Z = W_Q + W_K + W_V

LANES = 128
VMEM_LIMIT = 56 * 1024 * 1024

TM = 512
TN_IN = 2304
TN_OUT = 1024
TQ = 256
TKB = 1024
GS = 4
NP = 16

LOG2E = math.log2(math.e)
Q_SCALE = HEAD_DIM ** -0.5 * LOG2E

_NT = (((1,), (1,)), ((), ()))


def _cparams(n_axes, flags=None):
    return pltpu.CompilerParams(
        dimension_semantics=("arbitrary",) * n_axes,
        vmem_limit_bytes=VMEM_LIMIT,
        flags=flags,
    )


def _silu(z):
    return z * (1.0 / (1.0 + jnp.exp(-z)))


def _inproj_body(xp_ref, xs_ref, g_ref, w_ref, h_ref, xn_ref, *, n_prompt_tiles):
    i = pl.program_id(0)
    j = pl.program_id(1)

    def norm(x_ref):
        x = x_ref[...]
        ms = jnp.mean(x * x, axis=-1, keepdims=True)
        xn_ref[...] = (x * lax.rsqrt(ms + EPS) * g_ref[...]).astype(BF16)

    @pl.when((j == 0) & (i < n_prompt_tiles))
    def _():
        norm(xp_ref)

    @pl.when((j == 0) & (i >= n_prompt_tiles))
    def _():
        norm(xs_ref)

    h_ref[...] = jnp.dot(xn_ref[...], w_ref[...], preferred_element_type=F32)


def _inproj(xp, xs, s_blk, gain, w_bf16, n_rows):
    n_prompt_tiles = (n_rows - TM) // TM
    grid = (n_rows // TM, W_IN // TN_IN)
    return pl.pallas_call(
        functools.partial(_inproj_body, n_prompt_tiles=n_prompt_tiles),
        grid=grid,
        in_specs=[
            pl.BlockSpec((TM, D_MODEL), lambda i, j: (jnp.minimum(i, n_prompt_tiles - 1), 0)),
            pl.BlockSpec((TM, D_MODEL), lambda i, j: (s_blk, 0)),
            pl.BlockSpec((1, D_MODEL), lambda i, j: (0, 0)),
            pl.BlockSpec((D_MODEL, TN_IN), lambda i, j: (0, j)),
        ],
        out_specs=pl.BlockSpec((TM, TN_IN), lambda i, j: (i, j)),
        out_shape=jax.ShapeDtypeStruct((n_rows, W_IN), F32),
        scratch_shapes=[pltpu.VMEM((TM, D_MODEL), BF16)],
        compiler_params=_cparams(2),
        name="inproj",
    )(xp, xs, gain.reshape(1, D_MODEL), w_bf16)


TR = 256
CW = 256


def _qkpost_body(h_ref, cos_ref, sin_ref, qg_ref, kg_ref, bd_ref, q_ref, k_ref):
    cos = jnp.concatenate([cos_ref[...]] * (CW // LANES), axis=1)
    sin = jnp.concatenate([sin_ref[...]] * (CW // LANES), axis=1)
    bd = bd_ref[...]
    lane = lax.broadcasted_iota(jnp.int32, (TR, CW), 1)
    first_half = (lane & (HEAD_DIM - 1)) < (HEAD_DIM // 2)

    def normrope(c, gain):
        t = h_ref[:, c * CW:(c + 1) * CW]
        ss = jnp.dot((t * t).astype(BF16), bd, preferred_element_type=F32)
        tn = t * lax.rsqrt(ss * (1.0 / HEAD_DIM) + EPS) * gain
        partner = jnp.where(first_half,
                            pltpu.roll(tn, CW - HEAD_DIM // 2, 1),
                            pltpu.roll(tn, HEAD_DIM // 2, 1))
        return tn * cos + partner * sin

    qg = qg_ref[...]
    kg = kg_ref[...]
    for c in range(W_Q // CW):
        q_ref[:, c * CW:(c + 1) * CW] = (normrope(c, qg) * Q_SCALE).astype(BF16)
    for c in range(W_K // CW):
        k_ref[:, c * CW:(c + 1) * CW] = normrope(W_Q // CW + c, kg)


def _qkpost(h, cos, sin, q_gain, k_gain, n_rows):
    head = jnp.arange(CW, dtype=jnp.int32) // HEAD_DIM
    bd = (head[:, None] == head[None, :]).astype(BF16)
    qg = jnp.tile(q_gain.astype(F32), CW // HEAD_DIM).reshape(1, CW)
    kg = jnp.tile(k_gain.astype(F32), CW // HEAD_DIM).reshape(1, CW)
    return pl.pallas_call(
        _qkpost_body,
        grid=(n_rows // TR,),
        in_specs=[
            pl.BlockSpec((TR, W_Q + W_K), lambda i: (i, 0)),
            pl.BlockSpec((TR, LANES), lambda i: (i, 0)),
            pl.BlockSpec((TR, LANES), lambda i: (i, 0)),
            pl.BlockSpec((1, CW), lambda i: (0, 0)),
            pl.BlockSpec((1, CW), lambda i: (0, 0)),
            pl.BlockSpec((CW, CW), lambda i: (0, 0)),
        ],
        out_specs=[
            pl.BlockSpec((TR, W_Q), lambda i: (i, 0)),
            pl.BlockSpec((TR, W_K), lambda i: (i, 0)),
        ],
        out_shape=[
            jax.ShapeDtypeStruct((n_rows, W_Q), BF16),
            jax.ShapeDtypeStruct((n_rows, W_K), F32),
        ],
        compiler_params=_cparams(1),
        name="qkpost",
    )(h, cos, sin, qg, kg, bd)


def _half_split(x, lo_half, even):
    if even:
        a = jnp.where(lo_half, x, 0.0)
        return a, pltpu.roll(a, HEAD_DIM, 1)
    b = jnp.where(lo_half, 0.0, x)
    return pltpu.roll(b, HEAD_DIM, 1), b


def _win_prompt_body(sink_ref, q_ref, kp_ref, kc_ref, vp_ref, vc_ref,
                     z0_ref, z1_ref, z2_ref, z3_ref, o_ref):
    n = pl.program_id(0)
    z_refs = (z0_ref, z1_ref, z2_ref, z3_ref)
    nk = 2 * WINDOW
    lo_half = lax.broadcasted_iota(jnp.int32, (nk, LANES), 1) < HEAD_DIM
    rows = 4 * WINDOW
    qi = lax.broadcasted_iota(jnp.int32, (rows, nk), 0) & (WINDOW - 1)
    ki = lax.broadcasted_iota(jnp.int32, (rows, nk), 1)
    valid = (ki > qi) & (ki <= qi + WINDOW) & (ki >= jnp.where(n > 0, 0, WINDOW))

    ones_k = jnp.ones((nk, LANES), BF16)

    def softmax(s, sink):
        s = jnp.where(valid, s, NEG)
        m = jnp.maximum(jnp.broadcast_to(jnp.max(s, axis=-1, keepdims=True), sink.shape), sink)
        p = jnp.exp2(s - _rep(m, nk // LANES)).astype(BF16)
        l = jnp.dot(p, ones_k, preferred_element_type=F32) + jnp.exp2(sink - m)
        return p, 1.0 / l

    for pr in range(KV_A // 2):
        cs = slice(pr * LANES, (pr + 1) * LANES)
        kk = jnp.concatenate([kp_ref[:, cs], kc_ref[:, cs]], axis=0)
        vv = jnp.concatenate([vp_ref[:, cs], vc_ref[:, cs]], axis=0)
        for sub in range(2):
            kv = 2 * pr + sub
            ka, kb = _half_split(kk, lo_half, sub == 0)
            va, vb = _half_split(vv, lo_half, sub == 0)
            base = kv * G_A * HEAD_DIM
            q4 = jnp.concatenate(
                [q_ref[:, base + j * LANES: base + (j + 1) * LANES] for j in range(4)], axis=0)
            s_e = lax.dot_general(q4, ka.astype(BF16), _NT, preferred_element_type=F32)
            s_o = lax.dot_general(q4, kb.astype(BF16), _NT, preferred_element_type=F32)
            sink_e = jnp.concatenate(
                [jnp.full((WINDOW, LANES), sink_ref[kv * G_A + 2 * j] * LOG2E, F32)
                 for j in range(4)], axis=0)
            sink_o = jnp.concatenate(
                [jnp.full((WINDOW, LANES), sink_ref[kv * G_A + 2 * j + 1] * LOG2E, F32)
                 for j in range(4)], axis=0)
            p_e, r_e = softmax(s_e, sink_e)
            p_o, r_o = softmax(s_o, sink_o)
            o = (jnp.dot(p_e, va.astype(BF16), preferred_element_type=F32) * r_e
                 + jnp.dot(p_o, vb.astype(BF16), preferred_element_type=F32) * r_o)
            zr = z_refs[kv // 2]
            zb = (kv % 2) * 512
            for j in range(4):
                z = zr[:, zb + j * LANES: zb + (j + 1) * LANES]
                o_ref[:, base + j * LANES: base + (j + 1) * LANES] = (
                    o[j * WINDOW:(j + 1) * WINDOW] * _silu(z)).astype(BF16)


def _win_prompt(q, k, h, sinks, seq):
    nb = seq // WINDOW
    prev = lambda n: (jnp.maximum(n - 1, 0), 0)
    zspec = lambda c: pl.BlockSpec((WINDOW, 1024), lambda n: (n, COL_Z // 1024 + c))
    return pl.pallas_call(
        _win_prompt_body,
        grid=(nb,),
        in_specs=[
            pl.BlockSpec(memory_space=pltpu.SMEM),
            pl.BlockSpec((WINDOW, W_Q), lambda n: (n, 0)),
            pl.BlockSpec((WINDOW, W_K), prev),
            pl.BlockSpec((WINDOW, W_K), lambda n: (n, 0)),
            pl.BlockSpec((WINDOW, W_V), lambda n: (jnp.maximum(n - 1, 0), COL_V // W_V)),
            pl.BlockSpec((WINDOW, W_V), lambda n: (n, COL_V // W_V)),
            zspec(0), zspec(1), zspec(2), zspec(3),
        ],
        out_specs=pl.BlockSpec((WINDOW, W_Q), lambda n: (n, 0)),
        out_shape=jax.ShapeDtypeStruct((seq, W_Q), BF16),
        compiler_params=_cparams(1),
        name="win_prompt",
    )(sinks.astype(F32), q, k, k, h, h, h, h, h, h)


def _win_sample_body(qd_ref, sink_ref, ck_ref, cv_ref, kn_ref, vn_ref,
                     o_ref, ok_ref, ov_ref, kall_ref, vall_ref, *, dec_seq):
    keep = WINDOW - dec_seq
    ok_ref[0:keep, :] = ck_ref[dec_seq:WINDOW, :]
    ok_ref[keep:WINDOW, :] = kn_ref[...]
    ov_ref[0:keep, :] = cv_ref[dec_seq:WINDOW, :]
    ov_ref[keep:WINDOW, :] = vn_ref[...]

    nk = 2 * WINDOW
    for src, new, dst in ((ck_ref, kn_ref, kall_ref), (cv_ref, vn_ref, vall_ref)):
        dst[0:WINDOW, :] = src[...]
        dst[WINDOW:nk, :] = jnp.zeros((WINDOW, KV_A * HEAD_DIM), F32)
        dst[WINDOW:WINDOW + dec_seq, :] = new[...]
    kall = kall_ref[...]
    vall = vall_ref[...]
    rows = dec_seq * G_A
    all_rows = KV_A * rows
    t = (lax.broadcasted_iota(jnp.int32, (all_rows, nk), 0) & (rows - 1)) >> (G_A.bit_length() - 1)
    i = lax.broadcasted_iota(jnp.int32, (all_rows, nk), 1)
    valid = (i > t) & (i <= WINDOW + t)
    lo_k = lax.broadcasted_iota(jnp.int32, (nk, LANES), 1) < HEAD_DIM
    lo_o = lax.broadcasted_iota(jnp.int32, (rows, LANES), 1) < HEAD_DIM

    ss = []
    for kv in range(KV_A):
        kk = kall[:, (kv // 2) * LANES:(kv // 2 + 1) * LANES]
        km = jnp.where(lo_k if kv % 2 == 0 else jnp.logical_not(lo_k), kk, 0.0).astype(BF16)
        ss.append(lax.dot_general(qd_ref[kv], km, _NT, preferred_element_type=F32))
    s = jnp.where(valid, jnp.concatenate(ss, axis=0), NEG)
    sink = sink_ref[...] * LOG2E
    m = jnp.maximum(jnp.broadcast_to(jnp.max(s, axis=-1, keepdims=True), sink.shape), sink)
    p = jnp.exp2(s - _rep(m, nk // LANES)).astype(BF16)
    l = jnp.dot(p, jnp.ones((nk, LANES), BF16), preferred_element_type=F32) + jnp.exp2(sink - m)
    r = 1.0 / l
    for pr in range(KV_A // 2):
        vvb = vall[:, pr * LANES:(pr + 1) * LANES].astype(BF16)
        outs = []
        for sub in range(2):
            rs = slice((2 * pr + sub) * rows, (2 * pr + sub + 1) * rows)
            outs.append(jnp.dot(p[rs], vvb, preferred_element_type=F32) * r[rs])
        o_ref[pr] = jnp.where(lo_o, outs[0], outs[1])


def _win_sample(qd, sink_tab, ck, cv, kn, vn, dec_batch, dec_seq):
    rows = dec_seq * G_A
    wkv = KV_A * HEAD_DIM
    bspec = lambda shp: pl.BlockSpec((None,) + shp, lambda b: (b,) + (0,) * len(shp))
    return pl.pallas_call(
        functools.partial(_win_sample_body, dec_seq=dec_seq),
        grid=(dec_batch,),
        in_specs=[
            bspec((KV_A, rows, LANES)),
            pl.BlockSpec((KV_A * rows, LANES), lambda b: (0, 0)),
            bspec((WINDOW, wkv)), bspec((WINDOW, wkv)),
            bspec((dec_seq, wkv)), bspec((dec_seq, wkv)),
        ],
        out_specs=[
            bspec((KV_A // 2, rows, LANES)),
            bspec((WINDOW, wkv)), bspec((WINDOW, wkv)),
        ],
        out_shape=[
            jax.ShapeDtypeStruct((dec_batch, KV_A // 2, rows, LANES), F32),
            jax.ShapeDtypeStruct((dec_batch, WINDOW, wkv), F32),
            jax.ShapeDtypeStruct((dec_batch, WINDOW, wkv), F32),
        ],
        scratch_shapes=[pltpu.VMEM((2 * WINDOW, wkv), F32), pltpu.VMEM((2 * WINDOW, wkv), F32)],
        compiler_params=_cparams(1),
        name="win_sample",
    )(qd, sink_tab, ck, cv, kn, vn)


def _lambda(lq1_ref, lk1_ref, lq2_ref, lk2_ref, lam_init):
    a = jnp.sum(lq1_ref[...] * lk1_ref[...], axis=-1, keepdims=True)
    b = jnp.sum(lq2_ref[...] * lk2_ref[...], axis=-1, keepdims=True)
    return jnp.exp(a) - jnp.exp(b) + lam_init


def _subln(o, gain, lam_init):
    ms = jnp.mean(o * o, axis=-1, keepdims=True)
    return o * lax.rsqrt(ms + EPS) * gain * (1.0 - lam_init)


def _rep(x, n):
    return jnp.concatenate([x] * n, axis=1) if n > 1 else x


def _diff_prompt_body(lq1_ref, lk1_ref, lq2_ref, lk2_ref, sub_ref,
                      q_ref, k_ref, v_ref, z_ref, o_ref,
                      q8_ref, m_ref, l_ref, acc_ref, *, lam_init):
    qt = pl.program_id(2)
    rows = GS * TQ
    for g in range(GS):
        q8_ref[g * TQ:(g + 1) * TQ, :] = q_ref[:, g * LANES:(g + 1) * LANES]
    m_ref[...] = jnp.full(m_ref.shape, NEG, F32)
    l_ref[...] = jnp.zeros(l_ref.shape, F32)
    acc_ref[...] = jnp.zeros(acc_ref.shape, F32)

    def tile(koff, width, masked):
        kf = k_ref[pl.ds(koff, width), :]
        vb = v_ref[pl.ds(koff, width), :].astype(BF16)
        lo_half = lax.broadcasted_iota(jnp.int32, (width, LANES), 1) < HEAD_DIM
        kxs = (jnp.where(lo_half, kf, 0.0).astype(BF16),
               jnp.where(lo_half, 0.0, kf).astype(BF16))
        q8 = q8_ref[...]
        if masked:
            qi = lax.broadcasted_iota(jnp.int32, (rows, width), 0) & (TQ - 1)
            ki = lax.broadcasted_iota(jnp.int32, (rows, width), 1)
            causal = ki <= qi
        for mi in range(2):
            s = lax.dot_general(q8, kxs[mi], _NT, preferred_element_type=F32)
            if masked:
                s = jnp.where(causal, s, NEG)
            m_old = m_ref[mi]
            m_new = jnp.maximum(m_old, jnp.max(s, axis=-1, keepdims=True))
            alpha = jnp.exp2(m_old - m_new)
            p = jnp.exp2(s - _rep(m_new, width // LANES))
            l_ref[mi] = alpha * l_ref[mi] + jnp.sum(p, axis=-1, keepdims=True)
            acc_ref[mi] = alpha * acc_ref[mi] + jnp.dot(
                p.astype(BF16), vb, preferred_element_type=F32)
            m_ref[mi] = m_new

    n_bulk = (qt * TQ) // TKB

    def bulk(i, carry):
        tile(pl.multiple_of(i * TKB, TKB), TKB, False)
        return carry

    def small(i, carry):
        tile(pl.multiple_of(i * TQ, TQ), TQ, False)
        return carry

    lax.fori_loop(0, n_bulk, bulk, 0)
    lax.fori_loop(n_bulk * (TKB // TQ), qt, small, 0)
    tile(pl.multiple_of(qt * TQ, TQ), TQ, True)

    lam = _lambda(lq1_ref, lk1_ref, lq2_ref, lk2_ref, lam_init)
    o = acc_ref[0] * (1.0 / l_ref[0]) - lam * (acc_ref[1] * (1.0 / l_ref[1]))
    on = _subln(o, sub_ref[...], lam_init)
    for g in range(GS):
        z = z_ref[:, g * LANES:(g + 1) * LANES]
        o_ref[:, g * LANES:(g + 1) * LANES] = (on[g * TQ:(g + 1) * TQ] * _silu(z)).astype(BF16)


def _diff_prompt(q, k, h, lam_vecs, subln, lam_init, seq):
    gw = GS * VD_B
    n_hg = G_B // GS
    vec = pl.BlockSpec((1, HEAD_DIM), lambda c, g, t: (0, 0))
    return pl.pallas_call(
        functools.partial(_diff_prompt_body, lam_init=lam_init),
        grid=(KV_B, n_hg, seq // TQ),
        in_specs=[
            vec, vec, vec, vec,
            pl.BlockSpec((1, VD_B), lambda c, g, t: (0, 0)),
            pl.BlockSpec((TQ, gw), lambda c, g, t: (t, c * n_hg + g)),
            pl.BlockSpec((seq, VD_B), lambda c, g, t: (0, c)),
            pl.BlockSpec((seq, VD_B), lambda c, g, t: (0, COL_V // VD_B + c)),
            pl.BlockSpec((TQ, gw), lambda c, g, t: (t, COL_Z // gw + c * n_hg + g)),
        ],
        out_specs=pl.BlockSpec((TQ, gw), lambda c, g, t: (t, c * n_hg + g)),
        out_shape=jax.ShapeDtypeStruct((seq, W_Q), BF16),
        scratch_shapes=[
            pltpu.VMEM((GS * TQ, LANES), BF16),
            pltpu.VMEM((2, GS * TQ, LANES), F32),
            pltpu.VMEM((2, GS * TQ, LANES), F32),
            pltpu.VMEM((2, GS * TQ, VD_B), F32),
        ],
        compiler_params=_cparams(3),
        name="diff_prompt",
    )(*lam_vecs, subln.reshape(1, VD_B), q, k, h, h)


def _diff_sample_body(pt_ref, lq1_ref, lk1_ref, lq2_ref, lk2_ref, sub_ref,
                      qbd_ref, kn_ref, vn_ref, *rest, lam_init, dec_seq, n_chunks):
    kp_refs = rest[:NP]
    vp_refs = rest[NP:2 * NP]
    o_ref, m_ref, l_ref, acc_ref = rest[2 * NP:]
    c = pl.program_id(1)
    rows = dec_seq * G_B

    @pl.when(c == 0)
    def _():
        m_ref[...] = jnp.full(m_ref.shape, NEG, F32)
        l_ref[...] = jnp.zeros(l_ref.shape, F32)
        acc_ref[...] = jnp.zeros(acc_ref.shape, F32)

    nr = 2 * rows

    def update(ss, vbs):
        s = jnp.concatenate(ss, axis=0)
        m_old = m_ref[...]
        m_new = jnp.maximum(m_old, jnp.max(s, axis=-1, keepdims=True))
        alpha = jnp.exp2(m_old - m_new)
        p = jnp.exp2(s - _rep(m_new, s.shape[1] // LANES))
        l_ref[...] = alpha * l_ref[...] + jnp.sum(p, axis=-1, keepdims=True)
        pb = p.astype(BF16)
        pv = jnp.concatenate(
            [jnp.dot(pb[kv * nr:(kv + 1) * nr], vbs[kv], preferred_element_type=F32)
             for kv in range(KV_B)], axis=0)
        acc_ref[...] = alpha * acc_ref[...] + pv
        m_ref[...] = m_new

    ss, vbs = [], []
    for kv in range(KV_B):
        rs = pl.ds(kv, PAGE, stride=KV_B)
        kb = jnp.concatenate([r[rs, :] for r in kp_refs], axis=0).astype(BF16)
        vbs.append(jnp.concatenate([r[rs, :] for r in vp_refs], axis=0).astype(BF16))
        ss.append(lax.dot_general(qbd_ref[kv], kb, _NT, preferred_element_type=F32))
    update(ss, vbs)

    @pl.when(c == n_chunks - 1)
    def _():
        lam = _lambda(lq1_ref, lk1_ref, lq2_ref, lk2_ref, lam_init)
        r = lax.broadcasted_iota(jnp.int32, (nr, LANES), 0)
        t = (r & (rows - 1)) >> (G_B.bit_length() - 1)
        i = lax.broadcasted_iota(jnp.int32, (nr, LANES), 1)
        valid = i <= t
        zpad = jnp.zeros((LANES - 8, LANES), F32)
        ss, vbs = [], []
        for kv in range(KV_B):
            cs = slice(kv * LANES, (kv + 1) * LANES)
            kb = jnp.concatenate([kn_ref[:, cs], zpad], axis=0).astype(BF16)
            vbs.append(jnp.concatenate([vn_ref[:, cs], zpad], axis=0).astype(BF16))
            s = lax.dot_general(qbd_ref[kv], kb, _NT, preferred_element_type=F32)
            ss.append(jnp.where(valid, s, NEG))
        update(ss, vbs)
        on = acc_ref[...] * (1.0 / l_ref[...])
        for kv in range(KV_B):
            o = on[kv * nr:kv * nr + rows] - lam * on[kv * nr + rows:(kv + 1) * nr]
            o_ref[kv] = _subln(o, sub_ref[...], lam_init)


def _diff_sample(pt_flat, qbd, kn8, vn8, cache_k, cache_v, lam_vecs, subln, lam_init,
                 dec_batch, dec_seq, n_pages):
    rows = dec_seq * G_B
    n_chunks = n_pages // NP
    wkv = KV_B * VD_B
    vec = pl.BlockSpec((1, HEAD_DIM), lambda b, c, pt: (0, 0))

    def page_spec(i):
        return pl.BlockSpec((None, PAGE * KV_B, VD_B),
                            lambda b, c, pt: (pt[b * n_pages + c * NP + i], 0, 0))

    grid_spec = pltpu.PrefetchScalarGridSpec(
        num_scalar_prefetch=1,
        grid=(dec_batch, n_chunks),
        in_specs=[
            vec, vec, vec, vec,
            pl.BlockSpec((1, VD_B), lambda b, c, pt: (0, 0)),
            pl.BlockSpec((None, KV_B, 2 * rows, LANES), lambda b, c, pt: (b, 0, 0, 0)),
            pl.BlockSpec((None, 8, wkv), lambda b, c, pt: (b, 0, 0)),
            pl.BlockSpec((None, 8, wkv), lambda b, c, pt: (b, 0, 0)),
        ] + [page_spec(i) for i in range(NP)] + [page_spec(i) for i in range(NP)],
        out_specs=pl.BlockSpec((None, KV_B, rows, VD_B), lambda b, c, pt: (b, 0, 0, 0)),
        scratch_shapes=[
            pltpu.VMEM((KV_B * 2 * rows, LANES), F32),
            pltpu.VMEM((KV_B * 2 * rows, LANES), F32),
            pltpu.VMEM((KV_B * 2 * rows, VD_B), F32),
        ],
    )
    return pl.pallas_call(
        functools.partial(_diff_sample_body, lam_init=lam_init, dec_seq=dec_seq,
                          n_chunks=n_chunks),
        grid_spec=grid_spec,
        out_shape=jax.ShapeDtypeStruct((dec_batch, KV_B, rows, VD_B), F32),
        compiler_params=_cparams(2),
        name="diff_sample",
    )(pt_flat, *lam_vecs, subln.reshape(1, VD_B), qbd, kn8, vn8,
      *([cache_k] * NP), *([cache_v] * NP))


def _gate_body(o_ref, z_ref, g_ref):
    g_ref[...] = (o_ref[...] * _silu(z_ref[...])).astype(BF16)


def _gate_sample(o_s, h, row_blk):
    n = o_s.shape[0]
    return pl.pallas_call(
        _gate_body,
        grid=(W_Z // 1024,),
        in_specs=[
            pl.BlockSpec((n, 1024), lambda c: (0, c)),
            pl.BlockSpec((n, 1024), lambda c: (row_blk, COL_Z // 1024 + c)),
        ],
        out_specs=pl.BlockSpec((n, 1024), lambda c: (0, c)),
        out_shape=jax.ShapeDtypeStruct((n, W_Z), BF16),
        compiler_params=_cparams(1),
        name="gate_sample",
    )(o_s, h)


def _outproj_body(gp_ref, gs_ref, w_ref, xp_ref, xs_ref, y_ref, *, n_prompt_tiles):
    i = pl.program_id(0)

    @pl.when(i < n_prompt_tiles)
    def _():
        y_ref[...] = xp_ref[...] + jnp.dot(gp_ref[...], w_ref[...], preferred_element_type=F32)

    @pl.when(i >= n_prompt_tiles)
    def _():
        y_ref[...] = xs_ref[...] + jnp.dot(gs_ref[...], w_ref[...], preferred_element_type=F32)


def _outproj(gp, gs, w_bf16, xp, xs, s_blk, n_rows):
    n_prompt_tiles = (n_rows - TM) // TM
    last = n_prompt_tiles - 1
    return pl.pallas_call(
        functools.partial(_outproj_body, n_prompt_tiles=n_prompt_tiles),
        grid=(n_rows // TM, D_MODEL // TN_OUT),
        in_specs=[
            pl.BlockSpec((TM, W_Z), lambda i, j: (jnp.minimum(i, last), 0)),
            pl.BlockSpec((TM, W_Z), lambda i, j: (0, 0)),
            pl.BlockSpec((W_Z, TN_OUT), lambda i, j: (0, j)),
            pl.BlockSpec((TM, TN_OUT), lambda i, j: (jnp.minimum(i, last), j)),
            pl.BlockSpec((TM, TN_OUT), lambda i, j: (s_blk, j)),
        ],
        out_specs=pl.BlockSpec((TM, TN_OUT), lambda i, j: (i, j)),
        out_shape=jax.ShapeDtypeStruct((n_rows, D_MODEL), F32),
        compiler_params=_cparams(2),
        name="outproj",
    )(gp, gs, w_bf16, xp, xs)


def _rope_tables(seq, dec_batch, dec_seq, past_len):
    half = HEAD_DIM // 2
    inv_freq = 1.0 / (ROPE_THETA ** (jnp.arange(0, HEAD_DIM, 2, dtype=F32) / HEAD_DIM))
    pos = jnp.concatenate([
        jnp.arange(seq, dtype=jnp.int32),
        jnp.tile(past_len + jnp.arange(dec_seq, dtype=jnp.int32), dec_batch),
    ])
    ang = pos.astype(F32)[:, None] * inv_freq[None, :]
    cos = jnp.cos(ang)
    sin = jnp.sin(ang)
    reps = LANES // HEAD_DIM
    cos_t = jnp.concatenate([cos, cos] * reps, axis=1)
    sin_t = jnp.concatenate([-sin, sin] * reps, axis=1)
    assert cos_t.shape[1] == LANES and half * 2 == HEAD_DIM
    return cos_t, sin_t


def kernel(x_prompt, x_sample, cache_win_k, cache_win_v, cache_k, cache_v, page_table,
           norm_a, w_in_a, q_norm_a, k_norm_a, sinks_a, w_out_a,
           norm_b, w_in_b, q_norm_b, k_norm_b, lambda_q1, lambda_k1, lambda_q2, lambda_k2,
           subln_b, w_out_b):
    batch, seq, _ = x_prompt.shape
    dec_batch, dec_seq, _ = x_sample.shape
    n_pool = cache_k.shape[1]
    n_pages = page_table.shape[1]
    past_len = n_pages * cache_k.shape[2]
    n_dec = dec_batch * dec_seq
    n_rows = seq + n_dec
    assert batch == 1 and n_dec == TM and seq % TM == 0 and cache_k.shape[2] == PAGE
    assert cache_win_k.shape[2] == WINDOW and n_pages % NP == 0
    s_blk = seq // TM

    cos_t, sin_t = _rope_tables(seq, dec_batch, dec_seq, past_len)
    xp = x_prompt.reshape(seq, D_MODEL)
    xs = x_sample.reshape(n_dec, D_MODEL)

    h = _inproj(xp, xs, 0, norm_a[0], w_in_a[0].astype(BF16), n_rows)
    q, k = _qkpost(h, cos_t, sin_t, q_norm_a[0], k_norm_a[0], n_rows)
    v = h[:, COL_V:COL_V + W_V]

    gp = _win_prompt(q, k, h, sinks_a[0], seq)

    rows_a = dec_seq * G_A
    qs = q[seq:].reshape(dec_batch, dec_seq, KV_A, G_A, HEAD_DIM)
    qs = qs.transpose(0, 2, 1, 3, 4).reshape(dec_batch, KV_A, rows_a, HEAD_DIM)
    qd = jnp.concatenate([qs, qs], axis=-1)
    sink_tab = jnp.broadcast_to(sinks_a[0].astype(F32).reshape(KV_A, 1, G_A, 1),
                                (KV_A, dec_seq, G_A, LANES)).reshape(KV_A * rows_a, LANES)
    wkv_a = KV_A * HEAD_DIM
    kn = k[seq:].reshape(dec_batch, dec_seq, wkv_a)
    vn = v[seq:].reshape(dec_batch, dec_seq, wkv_a)
    o_s, win_k_s, win_v_s = _win_sample(
        qd, sink_tab,
        cache_win_k[0].reshape(dec_batch, WINDOW, wkv_a),
        cache_win_v[0].reshape(dec_batch, WINDOW, wkv_a),
        kn, vn, dec_batch, dec_seq)
    o_s = o_s.reshape(dec_batch, KV_A // 2, dec_seq, G_A, 2, HEAD_DIM)
    o_s = o_s.transpose(0, 2, 1, 4, 3, 5).reshape(n_dec, W_Q)
    gs = _gate_sample(o_s, h, s_blk)

    x1 = _outproj(gp, gs, w_out_a[0].astype(BF16), xp, xs, 0, n_rows)

    wl = min(WINDOW, seq)
    win_k_p = k[seq - wl:seq].reshape(1, batch, wl, KV_A, HEAD_DIM)
    win_v_p = v[seq - wl:seq].reshape(1, batch, wl, KV_A, HEAD_DIM)
    win_k_s = win_k_s.reshape(1, dec_batch, WINDOW, KV_A, HEAD_DIM)
    win_v_s = win_v_s.reshape(1, dec_batch, WINDOW, KV_A, HEAD_DIM)

    lam_init = 0.8 - 0.6 * math.exp(-0.3 * 1)
    lam_vecs = [a[0].astype(F32).reshape(1, HEAD_DIM)
                for a in (lambda_q1, lambda_k1, lambda_q2, lambda_k2)]

    h = _inproj(x1, x1, s_blk, norm_b[0], w_in_b[0].astype(BF16), n_rows)
    q, k = _qkpost(h, cos_t, sin_t, q_norm_b[0], k_norm_b[0], n_rows)
    v = h[:, COL_V:COL_V + W_V]

    gp = _diff_prompt(q, k, h, lam_vecs, subln_b[0].astype(F32), lam_init, seq)

    rows_b = dec_seq * G_B
    qs = q[seq:].reshape(dec_batch, dec_seq, KV_B, G_B, 2, HEAD_DIM)
    qs = qs.transpose(0, 2, 4, 1, 3, 5).reshape(dec_batch, KV_B, 2, rows_b, HEAD_DIM)
    zq = jnp.zeros_like(qs[:, :, 0])
    qbd = jnp.concatenate([
        jnp.concatenate([qs[:, :, 0], zq], axis=-1),
        jnp.concatenate([zq, qs[:, :, 1]], axis=-1),
    ], axis=2)
    wkv_b = KV_B * VD_B
    kn = k[seq:].reshape(dec_batch, dec_seq, wkv_b)
    vn = v[seq:].reshape(dec_batch, dec_seq, wkv_b)
    padn = jnp.zeros((dec_batch, 8 - dec_seq, wkv_b), F32)
    kn8 = jnp.concatenate([kn, padn], axis=1)
    vn8 = jnp.concatenate([vn, padn], axis=1)
    o_s = _diff_sample(
        page_table.reshape(-1).astype(jnp.int32), qbd, kn8, vn8,
        cache_k[0].reshape(n_pool, PAGE * KV_B, VD_B), cache_v[0].reshape(n_pool, PAGE * KV_B, VD_B),
        lam_vecs, subln_b[0].astype(F32), lam_init, dec_batch, dec_seq, n_pages)
    o_s = o_s.reshape(dec_batch, KV_B, dec_seq, G_B * VD_B)
    o_s = o_s.transpose(0, 2, 1, 3).reshape(n_dec, W_Q)
    gs = _gate_sample(o_s, h, s_blk)

    x2 = _outproj(gp, gs, w_out_b[0].astype(BF16), x1, x1, s_blk, n_rows)

    y_prompt = x2[:seq].reshape(batch, seq, D_MODEL)
    y_sample = x2[seq:].reshape(dec_batch, dec_seq, D_MODEL)
    kv_k_p = k[:seq].reshape(1, batch, seq // PAGE, PAGE, KV_B, 2 * HEAD_DIM)
    kv_v_p = v[:seq].reshape(1, batch, seq // PAGE, PAGE, KV_B, VD_B)
    kv_k_s = kn.reshape(1, dec_batch, dec_seq, KV_B, 2 * HEAD_DIM)
    kv_v_s = vn.reshape(1, dec_batch, dec_seq, KV_B, VD_B)

    return (y_prompt, y_sample, win_k_p, win_v_p, win_k_s, win_v_s,
            kv_k_p, kv_v_p, kv_k_s, kv_v_s)
```

```python
import functools
import math

import jax
import jax.numpy as jnp
from jax import lax
from jax.experimental import pallas as pl
from jax.experimental.pallas import tpu as pltpu

F32 = jnp.float32
BF16 = jnp.bfloat16

D_MODEL = 2048
HEAD_DIM = 64
ROPE_THETA = 10000.0
EPS = 1e-6
WINDOW = 128
KV_A = 8
G_A = 8
KV_B = 4
G_B = 8
VD_B = 128
PAGE = 128
NEG = -1e30

W_Q = 4096
W_K = 512
W_V = 512
W_Z = 4096
W_IN = W_Q + W_K + W_V + W_Z
COL_K = W_Q
COL_V = W_Q + W_K
COL_Z = W_Q + W_K + W_V

LANES = 128
VMEM_LIMIT = 56 * 1024 * 1024

TM = 512
TN_IN = 2304
TN_OUT = 1024
TQ = 256
TKB = 1024
GS = 4
NP = 16

LOG2E = math.log2(math.e)
Q_SCALE = HEAD_DIM ** -0.5 * LOG2E

_NT = (((1,), (1,)), ((), ()))


def _cparams(n_axes, flags=None):
    return pltpu.CompilerParams(
        dimension_semantics=("arbitrary",) * n_axes,
        vmem_limit_bytes=VMEM_LIMIT,
        flags=flags,
    )


def _silu(z):
    return z * (1.0 / (1.0 + jnp.exp(-z)))


def _inproj_body(xp_ref, xs_ref, g_ref, w_ref, h_ref, xn_ref, *, n_prompt_tiles):
    i = pl.program_id(0)
    j = pl.program_id(1)

    def norm(x_ref):
        x = x_ref[...]
        ms = jnp.mean(x * x, axis=-1, keepdims=True)
        xn_ref[...] = (x * lax.rsqrt(ms + EPS) * g_ref[...]).astype(BF16)

    @pl.when((j == 0) & (i < n_prompt_tiles))
    def _():
        norm(xp_ref)

    @pl.when((j == 0) & (i >= n_prompt_tiles))
    def _():
        norm(xs_ref)

    h_ref[...] = jnp.dot(xn_ref[...], w_ref[...], preferred_element_type=F32)


def _inproj(xp, xs, gain, w_bf16):
    seq, n_dec = xp.shape[0], xs.shape[0]
    assert n_dec == TM and seq % TM == 0
    n_prompt_tiles = seq // TM
    n_rows = seq + n_dec
    grid = (n_rows // TM, W_IN // TN_IN)
    return pl.pallas_call(
        functools.partial(_inproj_body, n_prompt_tiles=n_prompt_tiles),
        grid=grid,
        in_specs=[
            pl.BlockSpec((TM, D_MODEL), lambda i, j: (jnp.minimum(i, n_prompt_tiles - 1), 0)),
            pl.BlockSpec((TM, D_MODEL), lambda i, j: (0, 0)),
            pl.BlockSpec((1, D_MODEL), lambda i, j: (0, 0)),
            pl.BlockSpec((D_MODEL, TN_IN), lambda i, j: (0, j)),
        ],
        out_specs=pl.BlockSpec((TM, TN_IN), lambda i, j: (i, j)),
        out_shape=jax.ShapeDtypeStruct((n_rows, W_IN), F32),
        scratch_shapes=[pltpu.VMEM((TM, D_MODEL), BF16)],
        compiler_params=_cparams(2),
        name="inproj",
    )(xp, xs, gain.reshape(1, D_MODEL), w_bf16)


TR = 256
CW = 256


def _qkpost_body(h_ref, cos_ref, sin_ref, qg_ref, kg_ref, bd_ref, q_ref, k_ref, *paged_refs):
    cos = jnp.concatenate([cos_ref[...]] * (CW // LANES), axis=1)
    sin = jnp.concatenate([sin_ref[...]] * (CW // LANES), axis=1)
    bd = bd_ref[...]
    lane = lax.broadcasted_iota(jnp.int32, (TR, CW), 1)
    first_half = (lane & (HEAD_DIM - 1)) < (HEAD_DIM // 2)

    def normrope(c, gain):
        t = h_ref[:, c * CW:(c + 1) * CW]
        ss = jnp.dot((t * t).astype(BF16), bd, preferred_element_type=F32)
        tn = t * lax.rsqrt(ss * (1.0 / HEAD_DIM) + EPS) * gain
        partner = jnp.where(first_half,
                            pltpu.roll(tn, CW - HEAD_DIM // 2, 1),
                            pltpu.roll(tn, HEAD_DIM // 2, 1))
        return tn * cos + partner * sin

    qg = qg_ref[...]
    kg = kg_ref[...]
    for c in range(W_Q // CW):
        q_ref[:, c * CW:(c + 1) * CW] = (normrope(c, qg) * Q_SCALE).astype(BF16)
    for c in range(W_K // CW):
        kc = normrope(W_Q // CW + c, kg)
        k_ref[:, c * CW:(c + 1) * CW] = kc
        if paged_refs:
            for u in range(CW // VD_B):
                paged_refs[0][pl.ds(c * (CW // VD_B) + u, TR, stride=KV_B), :] = (
                    kc[:, u * VD_B:(u + 1) * VD_B])
    if paged_refs:
        for u in range(KV_B):
            paged_refs[1][pl.ds(u, TR, stride=KV_B), :] = (
                h_ref[:, COL_V + u * VD_B:COL_V + (u + 1) * VD_B])


def _qkpost(h, cos, sin, q_gain, k_gain, n_rows, paged):
    head = jnp.arange(CW, dtype=jnp.int32) // HEAD_DIM
    bd = (head[:, None] == head[None, :]).astype(BF16)
    qg = jnp.tile(q_gain.astype(F32), CW // HEAD_DIM).reshape(1, CW)
    kg = jnp.tile(k_gain.astype(F32), CW // HEAD_DIM).reshape(1, CW)
    in_w = W_Q + W_K + (W_V if paged else 0)
    paged_specs = [pl.BlockSpec((TR * KV_B, VD_B), lambda i: (i, 0))] * 2 if paged else []
    paged_shapes = [jax.ShapeDtypeStruct((n_rows * KV_B, VD_B), F32)] * 2 if paged else []
    return pl.pallas_call(
        _qkpost_body,
        grid=(n_rows // TR,),
        in_specs=[
            pl.BlockSpec((TR, in_w), lambda i: (i, 0)),
            pl.BlockSpec((TR, LANES), lambda i: (i, 0)),
            pl.BlockSpec((TR, LANES), lambda i: (i, 0)),
            pl.BlockSpec((1, CW), lambda i: (0, 0)),
            pl.BlockSpec((1, CW), lambda i: (0, 0)),
            pl.BlockSpec((CW, CW), lambda i: (0, 0)),
        ],
        out_specs=[
            pl.BlockSpec((TR, W_Q), lambda i: (i, 0)),
            pl.BlockSpec((TR, W_K), lambda i: (i, 0)),
        ] + paged_specs,
        out_shape=[
            jax.ShapeDtypeStruct((n_rows, W_Q), BF16),
            jax.ShapeDtypeStruct((n_rows, W_K), F32),
        ] + paged_shapes,
        compiler_params=_cparams(1),
        name="qkpost",
    )(h, cos, sin, qg, kg, bd)


def _half_split(x, lo_half, even):
    if even:
        a = jnp.where(lo_half, x, 0.0)
        return a, pltpu.roll(a, HEAD_DIM, 1)
    b = jnp.where(lo_half, 0.0, x)
    return pltpu.roll(b, HEAD_DIM, 1), b


def _win_prompt_body(sink_ref, q_ref, kp_ref, kc_ref, vp_ref, vc_ref,
                     z0_ref, z1_ref, z2_ref, z3_ref, o_ref):
    n = pl.program_id(0)
    z_refs = (z0_ref, z1_ref, z2_ref, z3_ref)
    nk = 2 * WINDOW
    lo_half = lax.broadcasted_iota(jnp.int32, (nk, LANES), 1) < HEAD_DIM
    rows = 4 * WINDOW
    qi = lax.broadcasted_iota(jnp.int32, (rows, nk), 0) & (WINDOW - 1)
    ki = lax.broadcasted_iota(jnp.int32, (rows, nk), 1)
    valid = (ki > qi) & (ki <= qi + WINDOW) & (ki >= jnp.where(n > 0, 0, WINDOW))

    ones_k = jnp.ones((nk, LANES), BF16)

    def softmax(s, sink):
        s = jnp.where(valid, s, NEG)
        m = jnp.maximum(jnp.broadcast_to(jnp.max(s, axis=-1, keepdims=True), sink.shape), sink)
        p = jnp.exp2(s - _rep(m, nk // LANES)).astype(BF16)
        l = jnp.dot(p, ones_k, preferred_element_type=F32) + jnp.exp2(sink - m)
        return p, 1.0 / l

    for pr in range(KV_A // 2):
        cs = slice(pr * LANES, (pr + 1) * LANES)
        kk = jnp.concatenate([kp_ref[:, cs], kc_ref[:, cs]], axis=0)
        vv = jnp.concatenate([vp_ref[:, cs], vc_ref[:, cs]], axis=0)
        for sub in range(2):
            kv = 2 * pr + sub
            ka, kb = _half_split(kk, lo_half, sub == 0)
            va, vb = _half_split(vv, lo_half, sub == 0)
            base = kv * G_A * HEAD_DIM
            q4 = jnp.concatenate(
                [q_ref[:, base + j * LANES: base + (j + 1) * LANES] for j in range(4)], axis=0)
            s_e = lax.dot_general(q4, ka.astype(BF16), _NT, preferred_element_type=F32)
            s_o = lax.dot_general(q4, kb.astype(BF16), _NT, preferred_element_type=F32)
            sink_e = jnp.concatenate(
                [jnp.full((WINDOW, LANES), sink_ref[kv * G_A + 2 * j] * LOG2E, F32)
                 for j in range(4)], axis=0)
            sink_o = jnp.concatenate(
                [jnp.full((WINDOW, LANES), sink_ref[kv * G_A + 2 * j + 1] * LOG2E, F32)
                 for j in range(4)], axis=0)
            p_e, r_e = softmax(s_e, sink_e)
            p_o, r_o = softmax(s_o, sink_o)
            o = (jnp.dot(p_e, va.astype(BF16), preferred_element_type=F32) * r_e
                 + jnp.dot(p_o, vb.astype(BF16), preferred_element_type=F32) * r_o)
            zr = z_refs[kv // 2]
            zb = (kv % 2) * 512
            for j in range(4):
                z = zr[:, zb + j * LANES: zb + (j + 1) * LANES]
                o_ref[:, base + j * LANES: base + (j + 1) * LANES] = (
                    o[j * WINDOW:(j + 1) * WINDOW] * _silu(z)).astype(BF16)


def _win_prompt(q, k, h, sinks, seq):
    nb = seq // WINDOW
    prev = lambda n: (jnp.maximum(n - 1, 0), 0)
    zspec = lambda c: pl.BlockSpec((WINDOW, 1024), lambda n: (n, COL_Z // 1024 + c))
    return pl.pallas_call(
        _win_prompt_body,
        grid=(nb,),
        in_specs=[
            pl.BlockSpec(memory_space=pltpu.SMEM),
            pl.BlockSpec((WINDOW, W_Q), lambda n: (n, 0)),
            pl.BlockSpec((WINDOW, W_K), prev),
            pl.BlockSpec((WINDOW, W_K), lambda n: (n, 0)),
            pl.BlockSpec((WINDOW, W_V), lambda n: (jnp.maximum(n - 1, 0), COL_V // W_V)),
            pl.BlockSpec((WINDOW, W_V), lambda n: (n, COL_V // W_V)),
            zspec(0), zspec(1), zspec(2), zspec(3),
        ],
        out_specs=pl.BlockSpec((WINDOW, W_Q), lambda n: (n, 0)),
        out_shape=jax.ShapeDtypeStruct((seq, W_Q), BF16),
        compiler_params=_cparams(1),
        name="win_prompt",
    )(sinks.astype(F32), q, k, k, h, h, h, h, h, h)


def _win_sample_body(qd_ref, sink_ref, ck_ref, cv_ref, kn_ref, vn_ref,
                     o_ref, ok_ref, ov_ref, kall_ref, vall_ref, *, dec_seq):
    keep = WINDOW - dec_seq
    ok_ref[0:keep, :] = ck_ref[dec_seq:WINDOW, :]
    ok_ref[keep:WINDOW, :] = kn_ref[...]
    ov_ref[0:keep, :] = cv_ref[dec_seq:WINDOW, :]
    ov_ref[keep:WINDOW, :] = vn_ref[...]

    nk = 2 * WINDOW
    for src, new, dst in ((ck_ref, kn_ref, kall_ref), (cv_ref, vn_ref, vall_ref)):
        dst[0:WINDOW, :] = src[...]
        dst[WINDOW:nk, :] = jnp.zeros((WINDOW, KV_A * HEAD_DIM), F32)
        dst[WINDOW:WINDOW + dec_seq, :] = new[...]
    kall = kall_ref[...]
    vall = vall_ref[...]
    rows = dec_seq * G_A
    all_rows = KV_A * rows
    t = (lax.broadcasted_iota(jnp.int32, (all_rows, nk), 0) & (rows - 1)) >> (G_A.bit_length() - 1)
    i = lax.broadcasted_iota(jnp.int32, (all_rows, nk), 1)
    valid = (i > t) & (i <= WINDOW + t)
    lo_k = lax.broadcasted_iota(jnp.int32, (nk, LANES), 1) < HEAD_DIM
    lo_o = lax.broadcasted_iota(jnp.int32, (rows, LANES), 1) < HEAD_DIM

    ss = []
    for kv in range(KV_A):
        kk = kall[:, (kv // 2) * LANES:(kv // 2 + 1) * LANES]
        km = jnp.where(lo_k if kv % 2 == 0 else jnp.logical_not(lo_k), kk, 0.0).astype(BF16)
        ss.append(lax.dot_general(qd_ref[kv], km, _NT, preferred_element_type=F32))
    s = jnp.where(valid, jnp.concatenate(ss, axis=0), NEG)
    sink = sink_ref[...] * LOG2E
    m = jnp.maximum(jnp.broadcast_to(jnp.max(s, axis=-1, keepdims=True), sink.shape), sink)
    p = jnp.exp2(s - _rep(m, nk // LANES)).astype(BF16)
    l = jnp.dot(p, jnp.ones((nk, LANES), BF16), preferred_element_type=F32) + jnp.exp2(sink - m)
    r = 1.0 / l
    for pr in range(KV_A // 2):
        vvb = vall[:, pr * LANES:(pr + 1) * LANES].astype(BF16)
        outs = []
        for sub in range(2):
            rs = slice((2 * pr + sub) * rows, (2 * pr + sub + 1) * rows)
            outs.append(jnp.dot(p[rs], vvb, preferred_element_type=F32) * r[rs])
        o_ref[pr] = jnp.where(lo_o, outs[0], outs[1])


def _win_sample(qd, sink_tab, ck, cv, kn, vn, dec_batch, dec_seq):
    rows = dec_seq * G_A
    wkv = KV_A * HEAD_DIM
    bspec = lambda shp: pl.BlockSpec((None,) + shp, lambda b: (b,) + (0,) * len(shp))
    return pl.pallas_call(
        functools.partial(_win_sample_body, dec_seq=dec_seq),
        grid=(dec_batch,),
        in_specs=[
            bspec((KV_A, rows, LANES)),
            pl.BlockSpec((KV_A * rows, LANES), lambda b: (0, 0)),
            bspec((WINDOW, wkv)), bspec((WINDOW, wkv)),
            bspec((dec_seq, wkv)), bspec((dec_seq, wkv)),
        ],
        out_specs=[
            bspec((KV_A // 2, rows, LANES)),
            bspec((WINDOW, wkv)), bspec((WINDOW, wkv)),
        ],
        out_shape=[
            jax.ShapeDtypeStruct((dec_batch, KV_A // 2, rows, LANES), F32),
            jax.ShapeDtypeStruct((dec_batch, WINDOW, wkv), F32),
            jax.ShapeDtypeStruct((dec_batch, WINDOW, wkv), F32),
        ],
        scratch_shapes=[pltpu.VMEM((2 * WINDOW, wkv), F32), pltpu.VMEM((2 * WINDOW, wkv), F32)],
        compiler_params=_cparams(1),
        name="win_sample",
    )(qd, sink_tab, ck, cv, kn, vn)


def _lambda(lq1_ref, lk1_ref, lq2_ref, lk2_ref, lam_init):
    a = jnp.sum(lq1_ref[...] * lk1_ref[...], axis=-1, keepdims=True)
    b = jnp.sum(lq2_ref[...] * lk2_ref[...], axis=-1, keepdims=True)
    return jnp.exp(a) - jnp.exp(b) + lam_init


def _subln(o, gain, lam_init):
    ms = jnp.mean(o * o, axis=-1, keepdims=True)
    return o * lax.rsqrt(ms + EPS) * gain * (1.0 - lam_init)


def _rep(x, n):
    return jnp.concatenate([x] * n, axis=1) if n > 1 else x


def _diff_prompt_body(lq1_ref, lk1_ref, lq2_ref, lk2_ref, sub_ref,
                      q_ref, k_ref, v_ref, z_ref, o_ref,
                      q8_ref, m_ref, l_ref, acc_ref, *, lam_init):
    qt = pl.program_id(2)
    rows = GS * TQ
    for g in range(GS):
        q8_ref[g * TQ:(g + 1) * TQ, :] = q_ref[:, g * LANES:(g + 1) * LANES]
    m_ref[...] = jnp.full(m_ref.shape, NEG, F32)
    l_ref[...] = jnp.zeros(l_ref.shape, F32)
    acc_ref[...] = jnp.zeros(acc_ref.shape, F32)

    def tile(koff, width, masked):
        kf = k_ref[pl.ds(koff, width), :]
        vb = v_ref[pl.ds(koff, width), :].astype(BF16)
        lo_half = lax.broadcasted_iota(jnp.int32, (width, LANES), 1) < HEAD_DIM
        kxs = (jnp.where(lo_half, kf, 0.0).astype(BF16),
               jnp.where(lo_half, 0.0, kf).astype(BF16))
        q8 = q8_ref[...]
        if masked:
            qi = lax.broadcasted_iota(jnp.int32, (rows, width), 0) & (TQ - 1)
            ki = lax.broadcasted_iota(jnp.int32, (rows, width), 1)
            causal = ki <= qi
        for mi in range(2):
            s = lax.dot_general(q8, kxs[mi], _NT, preferred_element_type=F32)
            if masked:
                s = jnp.where(causal, s, NEG)
            m_old = m_ref[mi]
            m_new = jnp.maximum(m_old, jnp.max(s, axis=-1, keepdims=True))
            alpha = jnp.exp2(m_old - m_new)
            p = jnp.exp2(s - _rep(m_new, width // LANES))
            l_ref[mi] = alpha * l_ref[mi] + jnp.sum(p, axis=-1, keepdims=True)
            acc_ref[mi] = alpha * acc_ref[mi] + jnp.dot(
                p.astype(BF16), vb, preferred_element_type=F32)
            m_ref[mi] = m_new

    n_bulk = (qt * TQ) // TKB

    def bulk(i, carry):
        tile(pl.multiple_of(i * TKB, TKB), TKB, False)
        return carry

    def small(i, carry):
        tile(pl.multiple_of(i * TQ, TQ), TQ, False)
        return carry

    lax.fori_loop(0, n_bulk, bulk, 0)
    lax.fori_loop(n_bulk * (TKB // TQ), qt, small, 0)
    tile(pl.multiple_of(qt * TQ, TQ), TQ, True)

    lam = _lambda(lq1_ref, lk1_ref, lq2_ref, lk2_ref, lam_init)
    o = acc_ref[0] * (1.0 / l_ref[0]) - lam * (acc_ref[1] * (1.0 / l_ref[1]))
    on = _subln(o, sub_ref[...], lam_init)
    for g in range(GS):
        z = z_ref[:, g * LANES:(g + 1) * LANES]
        o_ref[:, g * LANES:(g + 1) * LANES] = (on[g * TQ:(g + 1) * TQ] * _silu(z)).astype(BF16)


def _diff_prompt(q, k, h, lam_vecs, subln, lam_init, seq):
    gw = GS * VD_B
    n_hg = G_B // GS
    vec = pl.BlockSpec((1, HEAD_DIM), lambda c, g, t: (0, 0))
    return pl.pallas_call(
        functools.partial(_diff_prompt_body, lam_init=lam_init),
        grid=(KV_B, n_hg, seq // TQ),
        in_specs=[
            vec, vec, vec, vec,
            pl.BlockSpec((1, VD_B), lambda c, g, t: (0, 0)),
            pl.BlockSpec((TQ, gw), lambda c, g, t: (t, c * n_hg + g)),
            pl.BlockSpec((seq, VD_B), lambda c, g, t: (0, c)),
            pl.BlockSpec((seq, VD_B), lambda c, g, t: (0, COL_V // VD_B + c)),
            pl.BlockSpec((TQ, gw), lambda c, g, t: (t, COL_Z // gw + c * n_hg + g)),
        ],
        out_specs=pl.BlockSpec((TQ, gw), lambda c, g, t: (t, c * n_hg + g)),
        out_shape=jax.ShapeDtypeStruct((seq, W_Q), BF16),
        scratch_shapes=[
            pltpu.VMEM((GS * TQ, LANES), BF16),
            pltpu.VMEM((2, GS * TQ, LANES), F32),
            pltpu.VMEM((2, GS * TQ, LANES), F32),
            pltpu.VMEM((2, GS * TQ, VD_B), F32),
        ],
        compiler_params=_cparams(3),
        name="diff_prompt",
    )(*lam_vecs, subln.reshape(1, VD_B), q, k, h, h)


def _diff_sample_body(pt_ref, lq1_ref, lk1_ref, lq2_ref, lk2_ref, sub_ref,
                      qbd_ref, kn_ref, vn_ref, *rest, lam_init, dec_seq, n_chunks):
    kp_refs = rest[:NP]
    vp_refs = rest[NP:2 * NP]
    o_ref, m_ref, l_ref, acc_ref = rest[2 * NP:]
    c = pl.program_id(1)
    rows = dec_seq * G_B

    @pl.when(c == 0)
    def _():
        m_ref[...] = jnp.full(m_ref.shape, NEG, F32)
        l_ref[...] = jnp.zeros(l_ref.shape, F32)
        acc_ref[...] = jnp.zeros(acc_ref.shape, F32)

    nr = 2 * rows

    def update(ss, vbs):
        s = jnp.concatenate(ss, axis=0)
        m_old = m_ref[...]
        m_new = jnp.maximum(m_old, jnp.max(s, axis=-1, keepdims=True))
        alpha = jnp.exp2(m_old - m_new)
        pb = jnp.exp2(s - _rep(m_new, s.shape[1] // LANES)).astype(BF16)
        ones = jnp.ones((s.shape[1], LANES), BF16)
        pv = jnp.concatenate(
            [jnp.dot(pb[kv * nr:(kv + 1) * nr], jnp.concatenate([vbs[kv], ones], axis=1),
                     preferred_element_type=F32)
             for kv in range(KV_B)], axis=0)
        l_ref[...] = alpha * l_ref[...] + pv[:, VD_B:]
        acc_ref[...] = alpha * acc_ref[...] + pv[:, :VD_B]
        m_ref[...] = m_new

    ss, vbs = [], []
    for kv in range(KV_B):
        rs = pl.ds(kv, PAGE, stride=KV_B)
        kb = jnp.concatenate([r[rs, :] for r in kp_refs], axis=0).astype(BF16)
        vbs.append(jnp.concatenate([r[rs, :] for r in vp_refs], axis=0).astype(BF16))
        ss.append(lax.dot_general(qbd_ref[kv], kb, _NT, preferred_element_type=F32))
    update(ss, vbs)

    @pl.when(c == n_chunks - 1)
    def _():
        lam = _lambda(lq1_ref, lk1_ref, lq2_ref, lk2_ref, lam_init)
        r = lax.broadcasted_iota(jnp.int32, (nr, LANES), 0)
        t = (r & (rows - 1)) >> (G_B.bit_length() - 1)
        i = lax.broadcasted_iota(jnp.int32, (nr, LANES), 1)
        valid = i <= t
        zpad = jnp.zeros((LANES - 8, LANES), F32)
        ss, vbs = [], []
        for kv in range(KV_B):
            cs = slice(kv * LANES, (kv + 1) * LANES)
            kb = jnp.concatenate([kn_ref[:, cs], zpad], axis=0).astype(BF16)
            vbs.append(jnp.concatenate([vn_ref[:, cs], zpad], axis=0).astype(BF16))
            s = lax.dot_general(qbd_ref[kv], kb, _NT, preferred_element_type=F32)
            ss.append(jnp.where(valid, s, NEG))
        update(ss, vbs)
        on = acc_ref[...] * (1.0 / l_ref[...])
        for kv in range(KV_B):
            o = on[kv * nr:kv * nr + rows] - lam * on[kv * nr + rows:(kv + 1) * nr]
            o_ref[kv] = _subln(o, sub_ref[...], lam_init)


def _diff_sample(pt_flat, qbd, kn8, vn8, cache_k, cache_v, lam_vecs, subln, lam_init,
                 dec_batch, dec_seq, n_pages):
    rows = dec_seq * G_B
    n_chunks = n_pages // NP
    wkv = KV_B * VD_B
    vec = pl.BlockSpec((1, HEAD_DIM), lambda b, c, pt: (0, 0))

    def page_spec(i):
        return pl.BlockSpec((None, PAGE * KV_B, VD_B),
                            lambda b, c, pt: (pt[b * n_pages + c * NP + i], 0, 0))

    grid_spec = pltpu.PrefetchScalarGridSpec(
        num_scalar_prefetch=1,
        grid=(dec_batch, n_chunks),
        in_specs=[
            vec, vec, vec, vec,
            pl.BlockSpec((1, VD_B), lambda b, c, pt: (0, 0)),
            pl.BlockSpec((None, KV_B, 2 * rows, LANES), lambda b, c, pt: (b, 0, 0, 0)),
            pl.BlockSpec((None, 8, wkv), lambda b, c, pt: (b, 0, 0)),
            pl.BlockSpec((None, 8, wkv), lambda b, c, pt: (b, 0, 0)),
        ] + [page_spec(i) for i in range(NP)] + [page_spec(i) for i in range(NP)],
        out_specs=pl.BlockSpec((None, KV_B, rows, VD_B), lambda b, c, pt: (b, 0, 0, 0)),
        scratch_shapes=[
            pltpu.VMEM((KV_B * 2 * rows, LANES), F32),
            pltpu.VMEM((KV_B * 2 * rows, LANES), F32),
            pltpu.VMEM((KV_B * 2 * rows, VD_B), F32),
        ],
    )
    return pl.pallas_call(
        functools.partial(_diff_sample_body, lam_init=lam_init, dec_seq=dec_seq,
                          n_chunks=n_chunks),
        grid_spec=grid_spec,
        out_shape=jax.ShapeDtypeStruct((dec_batch, KV_B, rows, VD_B), F32),
        compiler_params=_cparams(2),
        name="diff_sample",
    )(pt_flat, *lam_vecs, subln.reshape(1, VD_B), qbd, kn8, vn8,
      *([cache_k] * NP), *([cache_v] * NP))


def _gate_body(o_ref, z_ref, g_ref):
    g_ref[...] = (o_ref[...] * _silu(z_ref[...])).astype(BF16)


def _gate_sample(o_s, h, row_blk):
    n = o_s.shape[0]
    return pl.pallas_call(
        _gate_body,
        grid=(W_Z // 1024,),
        in_specs=[
            pl.BlockSpec((n, 1024), lambda c: (0, c)),
            pl.BlockSpec((n, 1024), lambda c: (row_blk, COL_Z // 1024 + c)),
        ],
        out_specs=pl.BlockSpec((n, 1024), lambda c: (0, c)),
        out_shape=jax.ShapeDtypeStruct((n, W_Z), BF16),
        compiler_params=_cparams(1),
        name="gate_sample",
    )(o_s, h)


def _outproj_body(gp_ref, gs_ref, w_ref, xp_ref, xs_ref, yp_ref, ys_ref, *, n_prompt_tiles):
    i = pl.program_id(1)

    @pl.when(i < n_prompt_tiles)
    def _():
        yp_ref[...] = xp_ref[...] + jnp.dot(gp_ref[...], w_ref[...], preferred_element_type=F32)

    @pl.when(i >= n_prompt_tiles)
    def _():
        ys_ref[...] = xs_ref[...] + jnp.dot(gs_ref[...], w_ref[...], preferred_element_type=F32)


def _outproj(gp, gs, w_bf16, xp, xs):
    seq, n_dec = gp.shape[0], gs.shape[0]
    assert n_dec == TM and seq % TM == 0
    n_prompt_tiles = seq // TM
    last = n_prompt_tiles - 1
    return pl.pallas_call(
        functools.partial(_outproj_body, n_prompt_tiles=n_prompt_tiles),
        grid=(D_MODEL // TN_OUT, n_prompt_tiles + 1),
        in_specs=[
            pl.BlockSpec((TM, W_Z), lambda j, i: (jnp.minimum(i, last), 0)),
            pl.BlockSpec((TM, W_Z), lambda j, i: (0, 0)),
            pl.BlockSpec((W_Z, TN_OUT), lambda j, i: (0, j)),
            pl.BlockSpec((TM, TN_OUT), lambda j, i: (jnp.minimum(i, last), j)),
            pl.BlockSpec((TM, TN_OUT), lambda j, i: (0, j)),
        ],
        out_specs=[
            pl.BlockSpec((TM, TN_OUT), lambda j, i: (jnp.minimum(i, last), j)),
            pl.BlockSpec((TM, TN_OUT), lambda j, i: (0, j)),
        ],
        out_shape=[
            jax.ShapeDtypeStruct((seq, D_MODEL), F32),
            jax.ShapeDtypeStruct((n_dec, D_MODEL), F32),
        ],
        compiler_params=_cparams(2),
        name="outproj",
    )(gp, gs, w_bf16, xp, xs)


def _rope_tables(seq, dec_batch, dec_seq, past_len):
    half = HEAD_DIM // 2
    inv_freq = 1.0 / (ROPE_THETA ** (jnp.arange(0, HEAD_DIM, 2, dtype=F32) / HEAD_DIM))
    pos = jnp.concatenate([
        jnp.arange(seq, dtype=jnp.int32),
        jnp.tile(past_len + jnp.arange(dec_seq, dtype=jnp.int32), dec_batch),
    ])
    ang = pos.astype(F32)[:, None] * inv_freq[None, :]
    cos = jnp.cos(ang)
    sin = jnp.sin(ang)
    reps = LANES // HEAD_DIM
    cos_t = jnp.concatenate([cos, cos] * reps, axis=1)
    sin_t = jnp.concatenate([-sin, sin] * reps, axis=1)
    assert cos_t.shape[1] == LANES and half * 2 == HEAD_DIM
    return cos_t, sin_t


def kernel(x_prompt, x_sample, cache_win_k, cache_win_v, cache_k, cache_v, page_table,
           norm_a, w_in_a, q_norm_a, k_norm_a, sinks_a, w_out_a,
           norm_b, w_in_b, q_norm_b, k_norm_b, lambda_q1, lambda_k1, lambda_q2, lambda_k2,
           subln_b, w_out_b):
    batch, seq, _ = x_prompt.shape
    dec_batch, dec_seq, _ = x_sample.shape
    n_pool = cache_k.shape[1]
    n_pages = page_table.shape[1]
    past_len = n_pages * cache_k.shape[2]
    n_dec = dec_batch * dec_seq
    n_rows = seq + n_dec
    assert batch == 1 and n_dec == TM and seq % TM == 0 and cache_k.shape[2] == PAGE
    assert cache_win_k.shape[2] == WINDOW and n_pages % NP == 0
    s_blk = seq // TM

    cos_t, sin_t = _rope_tables(seq, dec_batch, dec_seq, past_len)
    xp = x_prompt.reshape(seq, D_MODEL)
    xs = x_sample.reshape(n_dec, D_MODEL)

    h = _inproj(xp, xs, norm_a[0], w_in_a[0].astype(BF16))
    q, k = _qkpost(h, cos_t, sin_t, q_norm_a[0], k_norm_a[0], n_rows, paged=False)
    v = h[:, COL_V:COL_V + W_V]

    gp = _win_prompt(q, k, h, sinks_a[0], seq)

    rows_a = dec_seq * G_A
    qs = q[seq:].reshape(dec_batch, dec_seq, KV_A, G_A, HEAD_DIM)
    qs = qs.transpose(0, 2, 1, 3, 4).reshape(dec_batch, KV_A, rows_a, HEAD_DIM)
    qd = jnp.concatenate([qs, qs], axis=-1)
    sink_tab = jnp.broadcast_to(sinks_a[0].astype(F32).reshape(KV_A, 1, G_A, 1),
                                (KV_A, dec_seq, G_A, LANES)).reshape(KV_A * rows_a, LANES)
    wkv_a = KV_A * HEAD_DIM
    kn = k[seq:].reshape(dec_batch, dec_seq, wkv_a)
    vn = v[seq:].reshape(dec_batch, dec_seq, wkv_a)
    o_s, win_k_s, win_v_s = _win_sample(
        qd, sink_tab,
        cache_win_k[0].reshape(dec_batch, WINDOW, wkv_a),
        cache_win_v[0].reshape(dec_batch, WINDOW, wkv_a),
        kn, vn, dec_batch, dec_seq)
    o_s = o_s.reshape(dec_batch, KV_A // 2, dec_seq, G_A, 2, HEAD_DIM)
    o_s = o_s.transpose(0, 2, 1, 4, 3, 5).reshape(n_dec, W_Q)
    gs = _gate_sample(o_s, h, s_blk)

    x1p, x1s = _outproj(gp, gs, w_out_a[0].astype(BF16), xp, xs)

    wl = min(WINDOW, seq)
    win_k_p = k[seq - wl:seq].reshape(1, batch, wl, KV_A, HEAD_DIM)
    win_v_p = v[seq - wl:seq].reshape(1, batch, wl, KV_A, HEAD_DIM)
    win_k_s = win_k_s.reshape(1, dec_batch, WINDOW, KV_A, HEAD_DIM)
    win_v_s = win_v_s.reshape(1, dec_batch, WINDOW, KV_A, HEAD_DIM)

    lam_init = 0.8 - 0.6 * math.exp(-0.3 * 1)
    lam_vecs = [a[0].astype(F32).reshape(1, HEAD_DIM)
                for a in (lambda_q1, lambda_k1, lambda_q2, lambda_k2)]

    h = _inproj(x1p, x1s, norm_b[0], w_in_b[0].astype(BF16))
    q, k, k_pg, v_pg = _qkpost(h, cos_t, sin_t, q_norm_b[0], k_norm_b[0], n_rows, paged=True)
    v = h[:, COL_V:COL_V + W_V]

    gp = _diff_prompt(q, k, h, lam_vecs, subln_b[0].astype(F32), lam_init, seq)

    rows_b = dec_seq * G_B
    qs = q[seq:].reshape(dec_batch, dec_seq, KV_B, G_B, 2, HEAD_DIM)
    qs = qs.transpose(0, 2, 4, 1, 3, 5).reshape(dec_batch, KV_B, 2, rows_b, HEAD_DIM)
    zq = jnp.zeros_like(qs[:, :, 0])
    qbd = jnp.concatenate([
        jnp.concatenate([qs[:, :, 0], zq], axis=-1),
        jnp.concatenate([zq, qs[:, :, 1]], axis=-1),
    ], axis=2)
    wkv_b = KV_B * VD_B
    kn = k[seq:].reshape(dec_batch, dec_seq, wkv_b)
    vn = v[seq:].reshape(dec_batch, dec_seq, wkv_b)
    padn = jnp.zeros((dec_batch, 8 - dec_seq, wkv_b), F32)
    kn8 = jnp.concatenate([kn, padn], axis=1)
    vn8 = jnp.concatenate([vn, padn], axis=1)
    o_s = _diff_sample(
        page_table.reshape(-1).astype(jnp.int32), qbd, kn8, vn8,
        cache_k[0].reshape(n_pool, PAGE * KV_B, VD_B), cache_v[0].reshape(n_pool, PAGE * KV_B, VD_B),
        lam_vecs, subln_b[0].astype(F32), lam_init, dec_batch, dec_seq, n_pages)
    o_s = o_s.reshape(dec_batch, KV_B, dec_seq, G_B * VD_B)
    o_s = o_s.transpose(0, 2, 1, 3).reshape(n_dec, W_Q)
    gs = _gate_sample(o_s, h, s_blk)

    x2p, x2s = _outproj(gp, gs, w_out_b[0].astype(BF16), x1p, x1s)

    y_prompt = x2p.reshape(batch, seq, D_MODEL)
    y_sample = x2s.reshape(dec_batch, dec_seq, D_MODEL)
    kv_k_p = k_pg[:seq * KV_B].reshape(1, batch, seq // PAGE, PAGE, KV_B, 2 * HEAD_DIM)
    kv_v_p = v_pg[:seq * KV_B].reshape(1, batch, seq // PAGE, PAGE, KV_B, VD_B)
    kv_k_s = k_pg[seq * KV_B:].reshape(1, dec_batch, dec_seq, KV_B, 2 * HEAD_DIM)
    kv_v_s = v_pg[seq * KV_B:].reshape(1, dec_batch, dec_seq, KV_B, VD_B)

    return (y_prompt, y_sample, win_k_p, win_v_p, win_k_s, win_v_s,
            kv_k_p, kv_v_p, kv_k_s, kv_v_s)
```

```python
import functools
import math

import jax
import jax.numpy as jnp
from jax import lax
from jax.experimental import pallas as pl
from jax.experimental.pallas import tpu as pltpu

F32 = jnp.float32
BF16 = jnp.bfloat16

D_MODEL = 2048
HEAD_DIM = 64
ROPE_THETA = 10000.0
EPS = 1e-6
WINDOW = 128
KV_A = 8
G_A = 8
KV_B = 4
G_B = 8
VD_B = 128
PAGE = 128
NEG = -1e30

W_Q = 4096
W_K = 512
W_V = 512
W_Z = 4096
W_IN = W_Q + W_K + W_V + W_Z
COL_K = W_Q
COL_V = W_Q + W_K
COL_Z = W_Q + W_K + W_V

LANES = 128
VMEM_LIMIT = 56 * 1024 * 1024

TM = 512
TN_IN = 2304
TN_OUT = 1024
TQ = 256
TKB = 1024
GS = 8
NP = 16

LOG2E = math.log2(math.e)
Q_SCALE = HEAD_DIM ** -0.5 * LOG2E

_NT = (((1,), (1,)), ((), ()))


def _cparams(n_axes, flags=None):
    return pltpu.CompilerParams(
        dimension_semantics=("arbitrary",) * n_axes,
        vmem_limit_bytes=VMEM_LIMIT,
        flags=flags,
    )


def _silu(z):
    return z * (1.0 / (1.0 + jnp.exp(-z)))


def _inproj_body(xp_ref, xs_ref, g_ref, w_ref, h_ref, xn_ref, *, n_prompt_tiles):
    i = pl.program_id(0)
    j = pl.program_id(1)

    def norm(x_ref):
        x = x_ref[...]
        ms = jnp.mean(x * x, axis=-1, keepdims=True)
        xn_ref[...] = (x * lax.rsqrt(ms + EPS) * g_ref[...]).astype(BF16)

    @pl.when((j == 0) & (i < n_prompt_tiles))
    def _():
        norm(xp_ref)

    @pl.when((j == 0) & (i >= n_prompt_tiles))
    def _():
        norm(xs_ref)

    h_ref[...] = jnp.dot(xn_ref[...], w_ref[...], preferred_element_type=F32)


def _inproj(xp, xs, gain, w_bf16):
    seq, n_dec = xp.shape[0], xs.shape[0]
    assert n_dec == TM and seq % TM == 0
    n_prompt_tiles = seq // TM
    n_rows = seq + n_dec
    grid = (n_rows // TM, W_IN // TN_IN)
    return pl.pallas_call(
        functools.partial(_inproj_body, n_prompt_tiles=n_prompt_tiles),
        grid=grid,
        in_specs=[
            pl.BlockSpec((TM, D_MODEL), lambda i, j: (jnp.minimum(i, n_prompt_tiles - 1), 0)),
            pl.BlockSpec((TM, D_MODEL), lambda i, j: (0, 0)),
            pl.BlockSpec((1, D_MODEL), lambda i, j: (0, 0)),
            pl.BlockSpec((D_MODEL, TN_IN), lambda i, j: (0, j)),
        ],
        out_specs=pl.BlockSpec((TM, TN_IN), lambda i, j: (i, j)),
        out_shape=jax.ShapeDtypeStruct((n_rows, W_IN), F32),
        scratch_shapes=[pltpu.VMEM((TM, D_MODEL), BF16)],
        compiler_params=_cparams(2),
        name="inproj",
    )(xp, xs, gain.reshape(1, D_MODEL), w_bf16)


TR = 256
CW = 256


def _qkpost_body(h_ref, cos_ref, sin_ref, qg_ref, kg_ref, bd_ref, q_ref, k_ref, *paged_refs):
    cos = jnp.concatenate([cos_ref[...]] * (CW // LANES), axis=1)
    sin = jnp.concatenate([sin_ref[...]] * (CW // LANES), axis=1)
    bd = bd_ref[...]
    lane = lax.broadcasted_iota(jnp.int32, (TR, CW), 1)
    first_half = (lane & (HEAD_DIM - 1)) < (HEAD_DIM // 2)

    def normrope(c, gain):
        t = h_ref[:, c * CW:(c + 1) * CW]
        ss = jnp.dot((t * t).astype(BF16), bd, preferred_element_type=F32)
        tn = t * lax.rsqrt(ss * (1.0 / HEAD_DIM) + EPS) * gain
        partner = jnp.where(first_half,
                            pltpu.roll(tn, CW - HEAD_DIM // 2, 1),
                            pltpu.roll(tn, HEAD_DIM // 2, 1))
        return tn * cos + partner * sin

    qg = qg_ref[...]
    kg = kg_ref[...]
    for c in range(W_Q // CW):
        q_ref[:, c * CW:(c + 1) * CW] = (normrope(c, qg) * Q_SCALE).astype(BF16)
    for c in range(W_K // CW):
        kc = normrope(W_Q // CW + c, kg)
        k_ref[:, c * CW:(c + 1) * CW] = kc
        if paged_refs:
            for u in range(CW // VD_B):
                paged_refs[0][pl.ds(c * (CW // VD_B) + u, TR, stride=KV_B), :] = (
                    kc[:, u * VD_B:(u + 1) * VD_B])
    if paged_refs:
        for u in range(KV_B):
            paged_refs[1][pl.ds(u, TR, stride=KV_B), :] = (
                h_ref[:, COL_V + u * VD_B:COL_V + (u + 1) * VD_B])


def _qkpost(h, cos, sin, q_gain, k_gain, n_rows, paged):
    head = jnp.arange(CW, dtype=jnp.int32) // HEAD_DIM
    bd = (head[:, None] == head[None, :]).astype(BF16)
    qg = jnp.tile(q_gain.astype(F32), CW // HEAD_DIM).reshape(1, CW)
    kg = jnp.tile(k_gain.astype(F32), CW // HEAD_DIM).reshape(1, CW)
    in_w = W_Q + W_K + (W_V if paged else 0)
    paged_specs = [pl.BlockSpec((TR * KV_B, VD_B), lambda i: (i, 0))] * 2 if paged else []
    paged_shapes = [jax.ShapeDtypeStruct((n_rows * KV_B, VD_B), F32)] * 2 if paged else []
    return pl.pallas_call(
        _qkpost_body,
        grid=(n_rows // TR,),
        in_specs=[
            pl.BlockSpec((TR, in_w), lambda i: (i, 0)),
            pl.BlockSpec((TR, LANES), lambda i: (i, 0)),
            pl.BlockSpec((TR, LANES), lambda i: (i, 0)),
            pl.BlockSpec((1, CW), lambda i: (0, 0)),
            pl.BlockSpec((1, CW), lambda i: (0, 0)),
            pl.BlockSpec((CW, CW), lambda i: (0, 0)),
        ],
        out_specs=[
            pl.BlockSpec((TR, W_Q), lambda i: (i, 0)),
            pl.BlockSpec((TR, W_K), lambda i: (i, 0)),
        ] + paged_specs,
        out_shape=[
            jax.ShapeDtypeStruct((n_rows, W_Q), BF16),
            jax.ShapeDtypeStruct((n_rows, W_K), F32),
        ] + paged_shapes,
        compiler_params=_cparams(1),
        name="qkpost",
    )(h, cos, sin, qg, kg, bd)


def _half_split(x, lo_half, even):
    if even:
        a = jnp.where(lo_half, x, 0.0)
        return a, pltpu.roll(a, HEAD_DIM, 1)
    b = jnp.where(lo_half, 0.0, x)
    return pltpu.roll(b, HEAD_DIM, 1), b


def _win_prompt_body(sink_ref, q_ref, kp_ref, kc_ref, vp_ref, vc_ref,
                     z0_ref, z1_ref, z2_ref, z3_ref, o_ref):
    n = pl.program_id(0)
    z_refs = (z0_ref, z1_ref, z2_ref, z3_ref)
    nk = 2 * WINDOW
    lo_half = lax.broadcasted_iota(jnp.int32, (nk, LANES), 1) < HEAD_DIM
    rows = 4 * WINDOW
    qi = lax.broadcasted_iota(jnp.int32, (rows, nk), 0) & (WINDOW - 1)
    ki = lax.broadcasted_iota(jnp.int32, (rows, nk), 1)
    valid = (ki > qi) & (ki <= qi + WINDOW) & (ki >= jnp.where(n > 0, 0, WINDOW))

    ones_k = jnp.ones((nk, LANES), BF16)

    def softmax(s, sink):
        s = jnp.where(valid, s, NEG)
        m = jnp.maximum(jnp.broadcast_to(jnp.max(s, axis=-1, keepdims=True), sink.shape), sink)
        p = jnp.exp2(s - _rep(m, nk // LANES)).astype(BF16)
        l = jnp.dot(p, ones_k, preferred_element_type=F32) + jnp.exp2(sink - m)
        return p, 1.0 / l

    for pr in range(KV_A // 2):
        cs = slice(pr * LANES, (pr + 1) * LANES)
        kk = jnp.concatenate([kp_ref[:, cs], kc_ref[:, cs]], axis=0)
        vv = jnp.concatenate([vp_ref[:, cs], vc_ref[:, cs]], axis=0)
        for sub in range(2):
            kv = 2 * pr + sub
            ka, kb = _half_split(kk, lo_half, sub == 0)
            va, vb = _half_split(vv, lo_half, sub == 0)
            base = kv * G_A * HEAD_DIM
            q4 = jnp.concatenate(
                [q_ref[:, base + j * LANES: base + (j + 1) * LANES] for j in range(4)], axis=0)
            s_e = lax.dot_general(q4, ka.astype(BF16), _NT, preferred_element_type=F32)
            s_o = lax.dot_general(q4, kb.astype(BF16), _NT, preferred_element_type=F32)
            sink_e = jnp.concatenate(
                [jnp.full((WINDOW, LANES), sink_ref[kv * G_A + 2 * j] * LOG2E, F32)
                 for j in range(4)], axis=0)
            sink_o = jnp.concatenate(
                [jnp.full((WINDOW, LANES), sink_ref[kv * G_A + 2 * j + 1] * LOG2E, F32)
                 for j in range(4)], axis=0)
            p_e, r_e = softmax(s_e, sink_e)
            p_o, r_o = softmax(s_o, sink_o)
            o = (jnp.dot(p_e, va.astype(BF16), preferred_element_type=F32) * r_e
                 + jnp.dot(p_o, vb.astype(BF16), preferred_element_type=F32) * r_o)
            zr = z_refs[kv // 2]
            zb = (kv % 2) * 512
            for j in range(4):
                z = zr[:, zb + j * LANES: zb + (j + 1) * LANES]
                o_ref[:, base + j * LANES: base + (j + 1) * LANES] = (
                    o[j * WINDOW:(j + 1) * WINDOW] * _silu(z)).astype(BF16)


def _win_prompt(q, k, h, sinks, seq):
    nb = seq // WINDOW
    prev = lambda n: (jnp.maximum(n - 1, 0), 0)
    zspec = lambda c: pl.BlockSpec((WINDOW, 1024), lambda n: (n, COL_Z // 1024 + c))
    return pl.pallas_call(
        _win_prompt_body,
        grid=(nb,),
        in_specs=[
            pl.BlockSpec(memory_space=pltpu.SMEM),
            pl.BlockSpec((WINDOW, W_Q), lambda n: (n, 0)),
            pl.BlockSpec((WINDOW, W_K), prev),
            pl.BlockSpec((WINDOW, W_K), lambda n: (n, 0)),
            pl.BlockSpec((WINDOW, W_V), lambda n: (jnp.maximum(n - 1, 0), COL_V // W_V)),
            pl.BlockSpec((WINDOW, W_V), lambda n: (n, COL_V // W_V)),
            zspec(0), zspec(1), zspec(2), zspec(3),
        ],
        out_specs=pl.BlockSpec((WINDOW, W_Q), lambda n: (n, 0)),
        out_shape=jax.ShapeDtypeStruct((seq, W_Q), BF16),
        compiler_params=_cparams(1),
        name="win_prompt",
    )(sinks.astype(F32), q, k, k, h, h, h, h, h, h)


def _win_sample_body(qd_ref, sink_ref, ck_ref, cv_ref, kn_ref, vn_ref,
                     o_ref, ok_ref, ov_ref, kall_ref, vall_ref, *, dec_seq):
    keep = WINDOW - dec_seq
    ok_ref[0:keep, :] = ck_ref[dec_seq:WINDOW, :]
    ok_ref[keep:WINDOW, :] = kn_ref[...]
    ov_ref[0:keep, :] = cv_ref[dec_seq:WINDOW, :]
    ov_ref[keep:WINDOW, :] = vn_ref[...]

    nk = 2 * WINDOW
    for src, new, dst in ((ck_ref, kn_ref, kall_ref), (cv_ref, vn_ref, vall_ref)):
        dst[0:WINDOW, :] = src[...]
        dst[WINDOW:nk, :] = jnp.zeros((WINDOW, KV_A * HEAD_DIM), F32)
        dst[WINDOW:WINDOW + dec_seq, :] = new[...]
    kall = kall_ref[...]
    vall = vall_ref[...]
    rows = dec_seq * G_A
    all_rows = KV_A * rows
    t = (lax.broadcasted_iota(jnp.int32, (all_rows, nk), 0) & (rows - 1)) >> (G_A.bit_length() - 1)
    i = lax.broadcasted_iota(jnp.int32, (all_rows, nk), 1)
    valid = (i > t) & (i <= WINDOW + t)
    lo_k = lax.broadcasted_iota(jnp.int32, (nk, LANES), 1) < HEAD_DIM
    lo_o = lax.broadcasted_iota(jnp.int32, (rows, LANES), 1) < HEAD_DIM

    ss = []
    for kv in range(KV_A):
        kk = kall[:, (kv // 2) * LANES:(kv // 2 + 1) * LANES]
        km = jnp.where(lo_k if kv % 2 == 0 else jnp.logical_not(lo_k), kk, 0.0).astype(BF16)
        ss.append(lax.dot_general(qd_ref[kv], km, _NT, preferred_element_type=F32))
    s = jnp.where(valid, jnp.concatenate(ss, axis=0), NEG)
    sink = sink_ref[...] * LOG2E
    m = jnp.maximum(jnp.broadcast_to(jnp.max(s, axis=-1, keepdims=True), sink.shape), sink)
    p = jnp.exp2(s - _rep(m, nk // LANES)).astype(BF16)
    l = jnp.dot(p, jnp.ones((nk, LANES), BF16), preferred_element_type=F32) + jnp.exp2(sink - m)
    r = 1.0 / l
    for pr in range(KV_A // 2):
        vvb = vall[:, pr * LANES:(pr + 1) * LANES].astype(BF16)
        outs = []
        for sub in range(2):
            rs = slice((2 * pr + sub) * rows, (2 * pr + sub + 1) * rows)
            outs.append(jnp.dot(p[rs], vvb, preferred_element_type=F32) * r[rs])
        o_ref[pr] = jnp.where(lo_o, outs[0], outs[1])


def _win_sample(qd, sink_tab, ck, cv, kn, vn, dec_batch, dec_seq):
    rows = dec_seq * G_A
    wkv = KV_A * HEAD_DIM
    bspec = lambda shp: pl.BlockSpec((None,) + shp, lambda b: (b,) + (0,) * len(shp))
    return pl.pallas_call(
        functools.partial(_win_sample_body, dec_seq=dec_seq),
        grid=(dec_batch,),
        in_specs=[
            bspec((KV_A, rows, LANES)),
            pl.BlockSpec((KV_A * rows, LANES), lambda b: (0, 0)),
            bspec((WINDOW, wkv)), bspec((WINDOW, wkv)),
            bspec((dec_seq, wkv)), bspec((dec_seq, wkv)),
        ],
        out_specs=[
            bspec((KV_A // 2, rows, LANES)),
            bspec((WINDOW, wkv)), bspec((WINDOW, wkv)),
        ],
        out_shape=[
            jax.ShapeDtypeStruct((dec_batch, KV_A // 2, rows, LANES), F32),
            jax.ShapeDtypeStruct((dec_batch, WINDOW, wkv), F32),
            jax.ShapeDtypeStruct((dec_batch, WINDOW, wkv), F32),
        ],
        scratch_shapes=[pltpu.VMEM((2 * WINDOW, wkv), F32), pltpu.VMEM((2 * WINDOW, wkv), F32)],
        compiler_params=_cparams(1),
        name="win_sample",
    )(qd, sink_tab, ck, cv, kn, vn)


def _lambda(lq1_ref, lk1_ref, lq2_ref, lk2_ref, lam_init):
    a = jnp.sum(lq1_ref[...] * lk1_ref[...], axis=-1, keepdims=True)
    b = jnp.sum(lq2_ref[...] * lk2_ref[...], axis=-1, keepdims=True)
    return jnp.exp(a) - jnp.exp(b) + lam_init


def _subln(o, gain, lam_init):
    ms = jnp.mean(o * o, axis=-1, keepdims=True)
    return o * lax.rsqrt(ms + EPS) * gain * (1.0 - lam_init)


def _rep(x, n):
    return jnp.concatenate([x] * n, axis=1) if n > 1 else x


def _diff_prompt_body(lq1_ref, lk1_ref, lq2_ref, lk2_ref, sub_ref,
                      q_ref, k_ref, v_ref, z_ref, o_ref,
                      q8_ref, m_ref, l_ref, acc_ref, *, lam_init):
    qt = pl.program_id(2)
    rows = GS * TQ
    for g in range(GS):
        q8_ref[g * TQ:(g + 1) * TQ, :] = q_ref[:, g * LANES:(g + 1) * LANES]
    m_ref[...] = jnp.full(m_ref.shape, NEG, F32)
    l_ref[...] = jnp.zeros(l_ref.shape, F32)
    acc_ref[...] = jnp.zeros(acc_ref.shape, F32)

    def tile(koff, width, masked):
        kf = k_ref[pl.ds(koff, width), :]
        vb = v_ref[pl.ds(koff, width), :].astype(BF16)
        lo_half = lax.broadcasted_iota(jnp.int32, (width, LANES), 1) < HEAD_DIM
        kxs = (jnp.where(lo_half, kf, 0.0).astype(BF16),
               jnp.where(lo_half, 0.0, kf).astype(BF16))
        q8 = q8_ref[...]
        if masked:
            qi = lax.broadcasted_iota(jnp.int32, (rows, width), 0) & (TQ - 1)
            ki = lax.broadcasted_iota(jnp.int32, (rows, width), 1)
            causal = ki <= qi
        for mi in range(2):
            s = lax.dot_general(q8, kxs[mi], _NT, preferred_element_type=F32)
            if masked:
                s = jnp.where(causal, s, NEG)
            m_old = m_ref[mi]
            m_new = jnp.maximum(m_old, jnp.max(s, axis=-1, keepdims=True))
            alpha = jnp.exp2(m_old - m_new)
            p = jnp.exp2(s - _rep(m_new, width // LANES))
            l_ref[mi] = alpha * l_ref[mi] + jnp.sum(p, axis=-1, keepdims=True)
            acc_ref[mi] = alpha * acc_ref[mi] + jnp.dot(
                p.astype(BF16), vb, preferred_element_type=F32)
            m_ref[mi] = m_new

    n_bulk = (qt * TQ) // TKB

    def bulk(i, carry):
        tile(pl.multiple_of(i * TKB, TKB), TKB, False)
        return carry

    def small(i, carry):
        tile(pl.multiple_of(i * TQ, TQ), TQ, False)
        return carry

    def bulk2(j, carry):
        tile(pl.multiple_of(2 * j * TKB, TKB), TKB, False)
        tile(pl.multiple_of((2 * j + 1) * TKB, TKB), TKB, False)
        return carry

    lax.fori_loop(0, n_bulk // 2, bulk2, 0)
    lax.fori_loop((n_bulk // 2) * 2, n_bulk, bulk, 0)
    lax.fori_loop(n_bulk * (TKB // TQ), qt, small, 0)
    tile(pl.multiple_of(qt * TQ, TQ), TQ, True)

    lam = _lambda(lq1_ref, lk1_ref, lq2_ref, lk2_ref, lam_init)
    o = acc_ref[0] * (1.0 / l_ref[0]) - lam * (acc_ref[1] * (1.0 / l_ref[1]))
    on = _subln(o, sub_ref[...], lam_init)
    for g in range(GS):
        z = z_ref[:, g * LANES:(g + 1) * LANES]
        o_ref[:, g * LANES:(g + 1) * LANES] = (on[g * TQ:(g + 1) * TQ] * _silu(z)).astype(BF16)


def _diff_prompt(q, k, h, lam_vecs, subln, lam_init, seq):
    gw = GS * VD_B
    n_hg = G_B // GS
    vec = pl.BlockSpec((1, HEAD_DIM), lambda c, g, t: (0, 0))
    return pl.pallas_call(
        functools.partial(_diff_prompt_body, lam_init=lam_init),
        grid=(KV_B, n_hg, seq // TQ),
        in_specs=[
            vec, vec, vec, vec,
            pl.BlockSpec((1, VD_B), lambda c, g, t: (0, 0)),
            pl.BlockSpec((TQ, gw), lambda c, g, t: (t, c * n_hg + g)),
            pl.BlockSpec((seq, VD_B), lambda c, g, t: (0, c)),
            pl.BlockSpec((seq, VD_B), lambda c, g, t: (0, COL_V // VD_B + c)),
            pl.BlockSpec((TQ, gw), lambda c, g, t: (t, COL_Z // gw + c * n_hg + g)),
        ],
        out_specs=pl.BlockSpec((TQ, gw), lambda c, g, t: (t, c * n_hg + g)),
        out_shape=jax.ShapeDtypeStruct((seq, W_Q), BF16),
        scratch_shapes=[
            pltpu.VMEM((GS * TQ, LANES), BF16),
            pltpu.VMEM((2, GS * TQ, LANES), F32),
            pltpu.VMEM((2, GS * TQ, LANES), F32),
            pltpu.VMEM((2, GS * TQ, VD_B), F32),
        ],
        compiler_params=_cparams(3),
        name="diff_prompt",
    )(*lam_vecs, subln.reshape(1, VD_B), q, k, h, h)


def _diff_sample_body(pt_ref, lq1_ref, lk1_ref, lq2_ref, lk2_ref, sub_ref,
                      qbd_ref, kn_ref, vn_ref, *rest, lam_init, dec_seq, n_chunks):
    kp_refs = rest[:NP]
    vp_refs = rest[NP:2 * NP]
    o_ref, m_ref, l_ref, acc_ref = rest[2 * NP:]
    c = pl.program_id(1)
    rows = dec_seq * G_B

    @pl.when(c == 0)
    def _():
        m_ref[...] = jnp.full(m_ref.shape, NEG, F32)
        l_ref[...] = jnp.zeros(l_ref.shape, F32)
        acc_ref[...] = jnp.zeros(acc_ref.shape, F32)

    nr = 2 * rows

    def update(ss, vbs):
        s = jnp.concatenate(ss, axis=0)
        m_old = m_ref[...]
        m_new = jnp.maximum(m_old, jnp.max(s, axis=-1, keepdims=True))
        alpha = jnp.exp2(m_old - m_new)
        pb = jnp.exp2(s - _rep(m_new, s.shape[1] // LANES)).astype(BF16)
        ones = jnp.ones((s.shape[1], LANES), BF16)
        pv = jnp.concatenate(
            [jnp.dot(pb[kv * nr:(kv + 1) * nr], jnp.concatenate([vbs[kv], ones], axis=1),
                     preferred_element_type=F32)
             for kv in range(KV_B)], axis=0)
        l_ref[...] = alpha * l_ref[...] + pv[:, VD_B:]
        acc_ref[...] = alpha * acc_ref[...] + pv[:, :VD_B]
        m_ref[...] = m_new

    ss, vbs = [], []
    for kv in range(KV_B):
        rs = pl.ds(kv, PAGE, stride=KV_B)
        kb = jnp.concatenate([r[rs, :] for r in kp_refs], axis=0).astype(BF16)
        vbs.append(jnp.concatenate([r[rs, :] for r in vp_refs], axis=0).astype(BF16))
        ss.append(lax.dot_general(qbd_ref[kv], kb, _NT, preferred_element_type=F32))
    update(ss, vbs)

    @pl.when(c == n_chunks - 1)
    def _():
        lam = _lambda(lq1_ref, lk1_ref, lq2_ref, lk2_ref, lam_init)
        r = lax.broadcasted_iota(jnp.int32, (nr, LANES), 0)
        t = (r & (rows - 1)) >> (G_B.bit_length() - 1)
        i = lax.broadcasted_iota(jnp.int32, (nr, LANES), 1)
        valid = i <= t
        zpad = jnp.zeros((LANES - 8, LANES), F32)
        ss, vbs = [], []
        for kv in range(KV_B):
            cs = slice(kv * LANES, (kv + 1) * LANES)
            kb = jnp.concatenate([kn_ref[:, cs], zpad], axis=0).astype(BF16)
            vbs.append(jnp.concatenate([vn_ref[:, cs], zpad], axis=0).astype(BF16))
            s = lax.dot_general(qbd_ref[kv], kb, _NT, preferred_element_type=F32)
            ss.append(jnp.where(valid, s, NEG))
        update(ss, vbs)
        on = acc_ref[...] * (1.0 / l_ref[...])
        for kv in range(KV_B):
            o = on[kv * nr:kv * nr + rows] - lam * on[kv * nr + rows:(kv + 1) * nr]
            o_ref[kv] = _subln(o, sub_ref[...], lam_init)


def _diff_sample(pt_flat, qbd, kn8, vn8, cache_k, cache_v, lam_vecs, subln, lam_init,
                 dec_batch, dec_seq, n_pages):
    rows = dec_seq * G_B
    n_chunks = n_pages // NP
    wkv = KV_B * VD_B
    vec = pl.BlockSpec((1, HEAD_DIM), lambda b, c, pt: (0, 0))

    def page_spec(i):
        return pl.BlockSpec((None, PAGE * KV_B, VD_B),
                            lambda b, c, pt: (pt[b * n_pages + c * NP + i], 0, 0))

    grid_spec = pltpu.PrefetchScalarGridSpec(
        num_scalar_prefetch=1,
        grid=(dec_batch, n_chunks),
        in_specs=[
            vec, vec, vec, vec,
            pl.BlockSpec((1, VD_B), lambda b, c, pt: (0, 0)),
            pl.BlockSpec((None, KV_B, 2 * rows, LANES), lambda b, c, pt: (b, 0, 0, 0)),
            pl.BlockSpec((None, 8, wkv), lambda b, c, pt: (b, 0, 0)),
            pl.BlockSpec((None, 8, wkv), lambda b, c, pt: (b, 0, 0)),
        ] + [page_spec(i) for i in range(NP)] + [page_spec(i) for i in range(NP)],
        out_specs=pl.BlockSpec((None, KV_B, rows, VD_B), lambda b, c, pt: (b, 0, 0, 0)),
        scratch_shapes=[
            pltpu.VMEM((KV_B * 2 * rows, LANES), F32),
            pltpu.VMEM((KV_B * 2 * rows, LANES), F32),
            pltpu.VMEM((KV_B * 2 * rows, VD_B), F32),
        ],
    )
    return pl.pallas_call(
        functools.partial(_diff_sample_body, lam_init=lam_init, dec_seq=dec_seq,
                          n_chunks=n_chunks),
        grid_spec=grid_spec,
        out_shape=jax.ShapeDtypeStruct((dec_batch, KV_B, rows, VD_B), F32),
        compiler_params=_cparams(2),
        name="diff_sample",
    )(pt_flat, *lam_vecs, subln.reshape(1, VD_B), qbd, kn8, vn8,
      *([cache_k] * NP), *([cache_v] * NP))


def _gate_body(o_ref, z_ref, g_ref):
    g_ref[...] = (o_ref[...] * _silu(z_ref[...])).astype(BF16)


def _gate_sample(o_s, h, row_blk):
    n = o_s.shape[0]
    return pl.pallas_call(
        _gate_body,
        grid=(W_Z // 1024,),
        in_specs=[
            pl.BlockSpec((n, 1024), lambda c: (0, c)),
            pl.BlockSpec((n, 1024), lambda c: (row_blk, COL_Z // 1024 + c)),
        ],
        out_specs=pl.BlockSpec((n, 1024), lambda c: (0, c)),
        out_shape=jax.ShapeDtypeStruct((n, W_Z), BF16),
        compiler_params=_cparams(1),
        name="gate_sample",
    )(o_s, h)


def _outproj_body(gp_ref, gs_ref, w_ref, xp_ref, xs_ref, yp_ref, ys_ref, *, n_prompt_tiles):
    i = pl.program_id(1)

    @pl.when(i < n_prompt_tiles)
    def _():
        yp_ref[...] = xp_ref[...] + jnp.dot(gp_ref[...], w_ref[...], preferred_element_type=F32)

    @pl.when(i >= n_prompt_tiles)
    def _():
        ys_ref[...] = xs_ref[...] + jnp.dot(gs_ref[...], w_ref[...], preferred_element_type=F32)


def _outproj(gp, gs, w_bf16, xp, xs):
    seq, n_dec = gp.shape[0], gs.shape[0]
    assert n_dec == TM and seq % TM == 0
    n_prompt_tiles = seq // TM
    last = n_prompt_tiles - 1
    return pl.pallas_call(
        functools.partial(_outproj_body, n_prompt_tiles=n_prompt_tiles),
        grid=(D_MODEL // TN_OUT, n_prompt_tiles + 1),
        in_specs=[
            pl.BlockSpec((TM, W_Z), lambda j, i: (jnp.minimum(i, last), 0)),
            pl.BlockSpec((TM, W_Z), lambda j, i: (0, 0)),
            pl.BlockSpec((W_Z, TN_OUT), lambda j, i: (0, j)),
            pl.BlockSpec((TM, TN_OUT), lambda j, i: (jnp.minimum(i, last), j)),
            pl.BlockSpec((TM, TN_OUT), lambda j, i: (0, j)),
        ],
        out_specs=[
            pl.BlockSpec((TM, TN_OUT), lambda j, i: (jnp.minimum(i, last), j)),
            pl.BlockSpec((TM, TN_OUT), lambda j, i: (0, j)),
        ],
        out_shape=[
            jax.ShapeDtypeStruct((seq, D_MODEL), F32),
            jax.ShapeDtypeStruct((n_dec, D_MODEL), F32),
        ],
        compiler_params=_cparams(2),
        name="outproj",
    )(gp, gs, w_bf16, xp, xs)


def _rope_tables(seq, dec_batch, dec_seq, past_len):
    half = HEAD_DIM // 2
    inv_freq = 1.0 / (ROPE_THETA ** (jnp.arange(0, HEAD_DIM, 2, dtype=F32) / HEAD_DIM))
    pos = jnp.concatenate([
        jnp.arange(seq, dtype=jnp.int32),
        jnp.tile(past_len + jnp.arange(dec_seq, dtype=jnp.int32), dec_batch),
    ])
    ang = pos.astype(F32)[:, None] * inv_freq[None, :]
    cos = jnp.cos(ang)
    sin = jnp.sin(ang)
    reps = LANES // HEAD_DIM
    cos_t = jnp.concatenate([cos, cos] * reps, axis=1)
    sin_t = jnp.concatenate([-sin, sin] * reps, axis=1)
    assert cos_t.shape[1] == LANES and half * 2 == HEAD_DIM
    return cos_t, sin_t


def kernel(x_prompt, x_sample, cache_win_k, cache_win_v, cache_k, cache_v, page_table,
           norm_a, w_in_a, q_norm_a, k_norm_a, sinks_a, w_out_a,
           norm_b, w_in_b, q_norm_b, k_norm_b, lambda_q1, lambda_k1, lambda_q2, lambda_k2,
           subln_b, w_out_b):
    batch, seq, _ = x_prompt.shape
    dec_batch, dec_seq, _ = x_sample.shape
    n_pool = cache_k.shape[1]
    n_pages = page_table.shape[1]
    past_len = n_pages * cache_k.shape[2]
    n_dec = dec_batch * dec_seq
    n_rows = seq + n_dec
    assert batch == 1 and n_dec == TM and seq % TM == 0 and cache_k.shape[2] == PAGE
    assert cache_win_k.shape[2] == WINDOW and n_pages % NP == 0
    s_blk = seq // TM

    cos_t, sin_t = _rope_tables(seq, dec_batch, dec_seq, past_len)
    xp = x_prompt.reshape(seq, D_MODEL)
    xs = x_sample.reshape(n_dec, D_MODEL)

    h = _inproj(xp, xs, norm_a[0], w_in_a[0].astype(BF16))
    q, k = _qkpost(h, cos_t, sin_t, q_norm_a[0], k_norm_a[0], n_rows, paged=False)
    v = h[:, COL_V:COL_V + W_V]

    gp = _win_prompt(q, k, h, sinks_a[0], seq)

    rows_a = dec_seq * G_A
    qs = q[seq:].reshape(dec_batch, dec_seq, KV_A, G_A, HEAD_DIM)
    qs = qs.transpose(0, 2, 1, 3, 4).reshape(dec_batch, KV_A, rows_a, HEAD_DIM)
    qd = jnp.concatenate([qs, qs], axis=-1)
    sink_tab = jnp.broadcast_to(sinks_a[0].astype(F32).reshape(KV_A, 1, G_A, 1),
                                (KV_A, dec_seq, G_A, LANES)).reshape(KV_A * rows_a, LANES)
    wkv_a = KV_A * HEAD_DIM
    kn = k[seq:].reshape(dec_batch, dec_seq, wkv_a)
    vn = v[seq:].reshape(dec_batch, dec_seq, wkv_a)
    o_s, win_k_s, win_v_s = _win_sample(
        qd, sink_tab,
        cache_win_k[0].reshape(dec_batch, WINDOW, wkv_a),
        cache_win_v[0].reshape(dec_batch, WINDOW, wkv_a),
        kn, vn, dec_batch, dec_seq)
    o_s = o_s.reshape(dec_batch, KV_A // 2, dec_seq, G_A, 2, HEAD_DIM)
    o_s = o_s.transpose(0, 2, 1, 4, 3, 5).reshape(n_dec, W_Q)
    gs = _gate_sample(o_s, h, s_blk)

    x1p, x1s = _outproj(gp, gs, w_out_a[0].astype(BF16), xp, xs)

    wl = min(WINDOW, seq)
    win_k_p = k[seq - wl:seq].reshape(1, batch, wl, KV_A, HEAD_DIM)
    win_v_p = v[seq - wl:seq].reshape(1, batch, wl, KV_A, HEAD_DIM)
    win_k_s = win_k_s.reshape(1, dec_batch, WINDOW, KV_A, HEAD_DIM)
    win_v_s = win_v_s.reshape(1, dec_batch, WINDOW, KV_A, HEAD_DIM)

    lam_init = 0.8 - 0.6 * math.exp(-0.3 * 1)
    lam_vecs = [a[0].astype(F32).reshape(1, HEAD_DIM)
                for a in (lambda_q1, lambda_k1, lambda_q2, lambda_k2)]

    h = _inproj(x1p, x1s, norm_b[0], w_in_b[0].astype(BF16))
    q, k, k_pg, v_pg = _qkpost(h, cos_t, sin_t, q_norm_b[0], k_norm_b[0], n_rows, paged=True)
    v = h[:, COL_V:COL_V + W_V]

    gp = _diff_prompt(q, k, h, lam_vecs, subln_b[0].astype(F32), lam_init, seq)

    rows_b = dec_seq * G_B
    qs = q[seq:].reshape(dec_batch, dec_seq, KV_B, G_B, 2, HEAD_DIM)
    qs = qs.transpose(0, 2, 4, 1, 3, 5).reshape(dec_batch, KV_B, 2, rows_b, HEAD_DIM)
    zq = jnp.zeros_like(qs[:, :, 0])
    qbd = jnp.concatenate([
        jnp.concatenate([qs[:, :, 0], zq], axis=-1),
        jnp.concatenate([zq, qs[:, :, 1]], axis=-1),
    ], axis=2)
    wkv_b = KV_B * VD_B
    kn = k[seq:].reshape(dec_batch, dec_seq, wkv_b)
    vn = v[seq:].reshape(dec_batch, dec_seq, wkv_b)
    padn = jnp.zeros((dec_batch, 8 - dec_seq, wkv_b), F32)
    kn8 = jnp.concatenate([kn, padn], axis=1)
    vn8 = jnp.concatenate([vn, padn], axis=1)
    o_s = _diff_sample(
        page_table.reshape(-1).astype(jnp.int32), qbd, kn8, vn8,
        cache_k[0].reshape(n_pool, PAGE * KV_B, VD_B), cache_v[0].reshape(n_pool, PAGE * KV_B, VD_B),
        lam_vecs, subln_b[0].astype(F32), lam_init, dec_batch, dec_seq, n_pages)
    o_s = o_s.reshape(dec_batch, KV_B, dec_seq, G_B * VD_B)
    o_s = o_s.transpose(0, 2, 1, 3).reshape(n_dec, W_Q)
    gs = _gate_sample(o_s, h, s_blk)

    x2p, x2s = _outproj(gp, gs, w_out_b[0].astype(BF16), x1p, x1s)

    y_prompt = x2p.reshape(batch, seq, D_MODEL)
    y_sample = x2s.reshape(dec_batch, dec_seq, D_MODEL)
    kv_k_p = k_pg[:seq * KV_B].reshape(1, batch, seq // PAGE, PAGE, KV_B, 2 * HEAD_DIM)
    kv_v_p = v_pg[:seq * KV_B].reshape(1, batch, seq // PAGE, PAGE, KV_B, VD_B)
    kv_k_s = k_pg[seq * KV_B:].reshape(1, dec_batch, dec_seq, KV_B, 2 * HEAD_DIM)
    kv_v_s = v_pg[seq * KV_B:].reshape(1, dec_batch, dec_seq, KV_B, VD_B)

    return (y_prompt, y_sample, win_k_p, win_v_p, win_k_s, win_v_s,
            kv_k_p, kv_v_p, kv_k_s, kv_v_s)
```

```python
import functools
import math

import jax
import jax.numpy as jnp
from jax import lax
from jax.experimental import pallas as pl
from jax.experimental.pallas import tpu as pltpu

F32 = jnp.float32
BF16 = jnp.bfloat16

D_MODEL = 2048
HEAD_DIM = 64
ROPE_THETA = 10000.0
EPS = 1e-6
WINDOW = 128
KV_A = 8
G_A = 8
KV_B = 4
G_B = 8
VD_B = 128
PAGE = 128
NEG = -1e30

W_Q = 4096
W_K = 512
W_V = 512
W_Z = 4096
W_IN = W_Q + W_K + W_V + W_Z
COL_K = W_Q
COL_V = W_Q + W_K
COL_Z = W_Q + W_K + W_V

LANES = 128
VMEM_LIMIT = 56 * 1024 * 1024

TM = 512
TN_IN = 2304
TN_OUT = 512
TQ = 256
TKB = 1024
GS = 8
NP = 32

LOG2E = math.log2(math.e)
Q_SCALE = HEAD_DIM ** -0.5 * LOG2E

_NT = (((1,), (1,)), ((), ()))


def _cparams(n_axes, flags=None):
    return pltpu.CompilerParams(
        dimension_semantics=("arbitrary",) * n_axes,
        vmem_limit_bytes=VMEM_LIMIT,
        flags=flags,
    )


def _silu(z):
    return z * (1.0 / (1.0 + jnp.exp(-z)))


def _inproj_body(xp_ref, xs_ref, g_ref, w_ref, h_ref, xn_ref, *, n_prompt_tiles):
    i = pl.program_id(0)
    j = pl.program_id(1)

    def norm(x_ref):
        x = x_ref[...]
        ms = jnp.mean(x * x, axis=-1, keepdims=True)
        xn_ref[...] = (x * lax.rsqrt(ms + EPS) * g_ref[...]).astype(BF16)

    @pl.when((j == 0) & (i < n_prompt_tiles))
    def _():
        norm(xp_ref)

    @pl.when((j == 0) & (i >= n_prompt_tiles))
    def _():
        norm(xs_ref)

    h_ref[...] = jnp.dot(xn_ref[...], w_ref[...], preferred_element_type=F32)


def _inproj(xp, xs, gain, w_bf16):
    seq, n_dec = xp.shape[0], xs.shape[0]
    assert n_dec == TM and seq % TM == 0
    n_prompt_tiles = seq // TM
    n_rows = seq + n_dec
    grid = (n_rows // TM, W_IN // TN_IN)
    return pl.pallas_call(
        functools.partial(_inproj_body, n_prompt_tiles=n_prompt_tiles),
        grid=grid,
        in_specs=[
            pl.BlockSpec((TM, D_MODEL), lambda i, j: (jnp.minimum(i, n_prompt_tiles - 1), 0)),
            pl.BlockSpec((TM, D_MODEL), lambda i, j: (0, 0)),
            pl.BlockSpec((1, D_MODEL), lambda i, j: (0, 0)),
            pl.BlockSpec((D_MODEL, TN_IN), lambda i, j: (0, j)),
        ],
        out_specs=pl.BlockSpec((TM, TN_IN), lambda i, j: (i, j)),
        out_shape=jax.ShapeDtypeStruct((n_rows, W_IN), F32),
        scratch_shapes=[pltpu.VMEM((TM, D_MODEL), BF16)],
        compiler_params=_cparams(2),
        name="inproj",
    )(xp, xs, gain.reshape(1, D_MODEL), w_bf16)


TR = 256
CW = 256


def _qkpost_body(h_ref, cos_ref, sin_ref, qg_ref, kg_ref, bd_ref, q_ref, k_ref, *paged_refs):
    cos = jnp.concatenate([cos_ref[...]] * (CW // LANES), axis=1)
    sin = jnp.concatenate([sin_ref[...]] * (CW // LANES), axis=1)
    bd = bd_ref[...]
    lane = lax.broadcasted_iota(jnp.int32, (TR, CW), 1)
    first_half = (lane & (HEAD_DIM - 1)) < (HEAD_DIM // 2)

    def normrope(c, gain):
        t = h_ref[:, c * CW:(c + 1) * CW]
        ss = jnp.dot((t * t).astype(BF16), bd, preferred_element_type=F32)
        tn = t * lax.rsqrt(ss * (1.0 / HEAD_DIM) + EPS) * gain
        partner = jnp.where(first_half,
                            pltpu.roll(tn, CW - HEAD_DIM // 2, 1),
                            pltpu.roll(tn, HEAD_DIM // 2, 1))
        return tn * cos + partner * sin

    qg = qg_ref[...]
    kg = kg_ref[...]
    for c in range(W_Q // CW):
        q_ref[:, c * CW:(c + 1) * CW] = (normrope(c, qg) * Q_SCALE).astype(BF16)
    for c in range(W_K // CW):
        kc = normrope(W_Q // CW + c, kg)
        k_ref[:, c * CW:(c + 1) * CW] = kc
        if paged_refs:
            for u in range(CW // VD_B):
                paged_refs[0][pl.ds(c * (CW // VD_B) + u, TR, stride=KV_B), :] = (
                    kc[:, u * VD_B:(u + 1) * VD_B])
    if paged_refs:
        for u in range(KV_B):
            paged_refs[1][pl.ds(u, TR, stride=KV_B), :] = (
                h_ref[:, COL_V + u * VD_B:COL_V + (u + 1) * VD_B])


def _qkpost(h, cos, sin, q_gain, k_gain, n_rows, paged):
    head = jnp.arange(CW, dtype=jnp.int32) // HEAD_DIM
    bd = (head[:, None] == head[None, :]).astype(BF16)
    qg = jnp.tile(q_gain.astype(F32), CW // HEAD_DIM).reshape(1, CW)
    kg = jnp.tile(k_gain.astype(F32), CW // HEAD_DIM).reshape(1, CW)
    in_w = W_Q + W_K + (W_V if paged else 0)
    paged_specs = [pl.BlockSpec((TR * KV_B, VD_B), lambda i: (i, 0))] * 2 if paged else []
    paged_shapes = [jax.ShapeDtypeStruct((n_rows * KV_B, VD_B), F32)] * 2 if paged else []
    return pl.pallas_call(
        _qkpost_body,
        grid=(n_rows // TR,),
        in_specs=[
            pl.BlockSpec((TR, in_w), lambda i: (i, 0)),
            pl.BlockSpec((TR, LANES), lambda i: (i, 0)),
            pl.BlockSpec((TR, LANES), lambda i: (i, 0)),
            pl.BlockSpec((1, CW), lambda i: (0, 0)),
            pl.BlockSpec((1, CW), lambda i: (0, 0)),
            pl.BlockSpec((CW, CW), lambda i: (0, 0)),
        ],
        out_specs=[
            pl.BlockSpec((TR, W_Q), lambda i: (i, 0)),
            pl.BlockSpec((TR, W_K), lambda i: (i, 0)),
        ] + paged_specs,
        out_shape=[
            jax.ShapeDtypeStruct((n_rows, W_Q), BF16),
            jax.ShapeDtypeStruct((n_rows, W_K), F32),
        ] + paged_shapes,
        compiler_params=_cparams(1),
        name="qkpost",
    )(h, cos, sin, qg, kg, bd)


def _half_split(x, lo_half, even):
    if even:
        a = jnp.where(lo_half, x, 0.0)
        return a, pltpu.roll(a, HEAD_DIM, 1)
    b = jnp.where(lo_half, 0.0, x)
    return pltpu.roll(b, HEAD_DIM, 1), b


def _win_prompt_body(sink_ref, q_ref, kp_ref, kc_ref, vp_ref, vc_ref,
                     z0_ref, z1_ref, z2_ref, z3_ref, o_ref):
    n = pl.program_id(0)
    z_refs = (z0_ref, z1_ref, z2_ref, z3_ref)
    nk = 2 * WINDOW
    lo_half = lax.broadcasted_iota(jnp.int32, (nk, LANES), 1) < HEAD_DIM
    rows = 4 * WINDOW
    qi = lax.broadcasted_iota(jnp.int32, (rows, nk), 0) & (WINDOW - 1)
    ki = lax.broadcasted_iota(jnp.int32, (rows, nk), 1)
    valid = (ki > qi) & (ki <= qi + WINDOW) & (ki >= jnp.where(n > 0, 0, WINDOW))

    ones_k = jnp.ones((nk, LANES), BF16)

    def softmax(s, sink):
        s = jnp.where(valid, s, NEG)
        m = jnp.maximum(jnp.broadcast_to(jnp.max(s, axis=-1, keepdims=True), sink.shape), sink)
        p = jnp.exp2(s - _rep(m, nk // LANES)).astype(BF16)
        l = jnp.dot(p, ones_k, preferred_element_type=F32) + jnp.exp2(sink - m)
        return p, 1.0 / l

    for pr in range(KV_A // 2):
        cs = slice(pr * LANES, (pr + 1) * LANES)
        kk = jnp.concatenate([kp_ref[:, cs], kc_ref[:, cs]], axis=0)
        vv = jnp.concatenate([vp_ref[:, cs], vc_ref[:, cs]], axis=0)
        for sub in range(2):
            kv = 2 * pr + sub
            ka, kb = _half_split(kk, lo_half, sub == 0)
            va, vb = _half_split(vv, lo_half, sub == 0)
            base = kv * G_A * HEAD_DIM
            q4 = jnp.concatenate(
                [q_ref[:, base + j * LANES: base + (j + 1) * LANES] for j in range(4)], axis=0)
            s_e = lax.dot_general(q4, ka.astype(BF16), _NT, preferred_element_type=F32)
            s_o = lax.dot_general(q4, kb.astype(BF16), _NT, preferred_element_type=F32)
            sink_e = jnp.concatenate(
                [jnp.full((WINDOW, LANES), sink_ref[kv * G_A + 2 * j] * LOG2E, F32)
                 for j in range(4)], axis=0)
            sink_o = jnp.concatenate(
                [jnp.full((WINDOW, LANES), sink_ref[kv * G_A + 2 * j + 1] * LOG2E, F32)
                 for j in range(4)], axis=0)
            p_e, r_e = softmax(s_e, sink_e)
            p_o, r_o = softmax(s_o, sink_o)
            o = (jnp.dot(p_e, va.astype(BF16), preferred_element_type=F32) * r_e
                 + jnp.dot(p_o, vb.astype(BF16), preferred_element_type=F32) * r_o)
            zr = z_refs[kv // 2]
            zb = (kv % 2) * 512
            for j in range(4):
                z = zr[:, zb + j * LANES: zb + (j + 1) * LANES]
                o_ref[:, base + j * LANES: base + (j + 1) * LANES] = (
                    o[j * WINDOW:(j + 1) * WINDOW] * _silu(z)).astype(BF16)


def _win_prompt(q, k, h, sinks, seq):
    nb = seq // WINDOW
    prev = lambda n: (jnp.maximum(n - 1, 0), 0)
    zspec = lambda c: pl.BlockSpec((WINDOW, 1024), lambda n: (n, COL_Z // 1024 + c))
    return pl.pallas_call(
        _win_prompt_body,
        grid=(nb,),
        in_specs=[
            pl.BlockSpec(memory_space=pltpu.SMEM),
            pl.BlockSpec((WINDOW, W_Q), lambda n: (n, 0)),
            pl.BlockSpec((WINDOW, W_K), prev),
            pl.BlockSpec((WINDOW, W_K), lambda n: (n, 0)),
            pl.BlockSpec((WINDOW, W_V), lambda n: (jnp.maximum(n - 1, 0), COL_V // W_V)),
            pl.BlockSpec((WINDOW, W_V), lambda n: (n, COL_V // W_V)),
            zspec(0), zspec(1), zspec(2), zspec(3),
        ],
        out_specs=pl.BlockSpec((WINDOW, W_Q), lambda n: (n, 0)),
        out_shape=jax.ShapeDtypeStruct((seq, W_Q), BF16),
        compiler_params=_cparams(1),
        name="win_prompt",
    )(sinks.astype(F32), q, k, k, h, h, h, h, h, h)


def _win_sample_body(qd_ref, sink_ref, ck_ref, cv_ref, kn_ref, vn_ref,
                     o_ref, ok_ref, ov_ref, kpad_ref, vpad_ref, *, dec_seq):
    keep = WINDOW - dec_seq
    wkv = KV_A * HEAD_DIM
    lane = lax.broadcasted_iota(jnp.int32, (wkv, WINDOW), 1)

    for pad, new in ((kpad_ref, kn_ref), (vpad_ref, vn_ref)):
        pad[...] = jnp.zeros(pad.shape, F32)
        pad[0:dec_seq, :] = new[...]
    kpad = kpad_ref[...]
    vpad = vpad_ref[...]

    def rolled(old_ref, new_pad):
        return jnp.where(lane >= keep, pltpu.roll(new_pad.T, keep, 1),
                         pltpu.roll(old_ref[...], keep, 1))

    ok_ref[...] = rolled(ck_ref, kpad)
    ov_ref[...] = rolled(cv_ref, vpad)

    rows = dec_seq * G_A
    all_rows = KV_A * rows
    t = (lax.broadcasted_iota(jnp.int32, (all_rows, WINDOW), 0) & (rows - 1)) >> (
        G_A.bit_length() - 1)
    i = lax.broadcasted_iota(jnp.int32, (all_rows, WINDOW), 1)
    valid_old = i > t
    valid_new = i <= t
    first = lax.broadcasted_iota(jnp.int32, (LANES, WINDOW), 0) < HEAD_DIM
    lo_k = lax.broadcasted_iota(jnp.int32, (LANES, LANES), 1) < HEAD_DIM
    lo_o = lax.broadcasted_iota(jnp.int32, (rows, LANES), 1) < HEAD_DIM

    s_old, s_new = [], []
    for kv in range(KV_A):
        ps = slice((kv // 2) * LANES, (kv // 2 + 1) * LANES)
        even = kv % 2 == 0
        q = qd_ref[kv]
        s_old.append(jnp.dot(
            q, jnp.where(first if even else jnp.logical_not(first), ck_ref[ps, :], 0.0).astype(BF16),
            preferred_element_type=F32))
        s_new.append(lax.dot_general(
            q, jnp.where(lo_k if even else jnp.logical_not(lo_k), kpad[:, ps], 0.0).astype(BF16),
            _NT, preferred_element_type=F32))
    s = jnp.concatenate([jnp.where(valid_old, jnp.concatenate(s_old, axis=0), NEG),
                         jnp.where(valid_new, jnp.concatenate(s_new, axis=0), NEG)], axis=1)
    sink = sink_ref[...] * LOG2E
    m = jnp.maximum(jnp.broadcast_to(jnp.max(s, axis=-1, keepdims=True), sink.shape), sink)
    p = jnp.exp2(s - _rep(m, 2)).astype(BF16)
    l = jnp.dot(p, jnp.ones((2 * WINDOW, LANES), BF16), preferred_element_type=F32) + jnp.exp2(
        sink - m)
    r = 1.0 / l
    for pr in range(KV_A // 2):
        ps = slice(pr * LANES, (pr + 1) * LANES)
        vo = cv_ref[ps, :].astype(BF16)
        vn = vpad[:, ps].astype(BF16)
        outs = []
        for sub in range(2):
            rs = slice((2 * pr + sub) * rows, (2 * pr + sub + 1) * rows)
            pv = (lax.dot_general(p[rs, :WINDOW], vo, _NT, preferred_element_type=F32)
                  + jnp.dot(p[rs, WINDOW:], vn, preferred_element_type=F32))
            outs.append(pv * r[rs])
        o_ref[pr] = jnp.where(lo_o, outs[0], outs[1])


def _win_sample(qd, sink_tab, ck, cv, kn, vn, dec_batch, dec_seq):
    rows = dec_seq * G_A
    wkv = KV_A * HEAD_DIM
    bspec = lambda shp: pl.BlockSpec((None,) + shp, lambda b: (b,) + (0,) * len(shp))
    return pl.pallas_call(
        functools.partial(_win_sample_body, dec_seq=dec_seq),
        grid=(dec_batch,),
        in_specs=[
            bspec((KV_A, rows, LANES)),
            pl.BlockSpec((KV_A * rows, LANES), lambda b: (0, 0)),
            bspec((wkv, WINDOW)), bspec((wkv, WINDOW)),
            bspec((dec_seq, wkv)), bspec((dec_seq, wkv)),
        ],
        out_specs=[
            bspec((KV_A // 2, rows, LANES)),
            bspec((wkv, WINDOW)), bspec((wkv, WINDOW)),
        ],
        out_shape=[
            jax.ShapeDtypeStruct((dec_batch, KV_A // 2, rows, LANES), F32),
            jax.ShapeDtypeStruct((dec_batch, wkv, WINDOW), F32),
            jax.ShapeDtypeStruct((dec_batch, wkv, WINDOW), F32),
        ],
        scratch_shapes=[pltpu.VMEM((LANES, wkv), F32), pltpu.VMEM((LANES, wkv), F32)],
        compiler_params=_cparams(1),
        name="win_sample",
    )(qd, sink_tab, ck, cv, kn, vn)


def _lambda(lq1_ref, lk1_ref, lq2_ref, lk2_ref, lam_init):
    a = jnp.sum(lq1_ref[...] * lk1_ref[...], axis=-1, keepdims=True)
    b = jnp.sum(lq2_ref[...] * lk2_ref[...], axis=-1, keepdims=True)
    return jnp.exp(a) - jnp.exp(b) + lam_init


def _subln(o, gain, lam_init):
    ms = jnp.mean(o * o, axis=-1, keepdims=True)
    return o * lax.rsqrt(ms + EPS) * gain * (1.0 - lam_init)


def _rep(x, n):
    return jnp.concatenate([x] * n, axis=1) if n > 1 else x


def _diff_prompt_body(lq1_ref, lk1_ref, lq2_ref, lk2_ref, sub_ref,
                      q_ref, k_ref, v_ref, z_ref, o_ref,
                      q8_ref, m_ref, l_ref, acc_ref, *, lam_init):
    qt = pl.program_id(2)
    rows = GS * TQ
    for g in range(GS):
        q8_ref[g * TQ:(g + 1) * TQ, :] = q_ref[:, g * LANES:(g + 1) * LANES]
    m_ref[...] = jnp.full(m_ref.shape, NEG, F32)
    l_ref[...] = jnp.zeros(l_ref.shape, F32)
    acc_ref[...] = jnp.zeros(acc_ref.shape, F32)

    def tile(koff, width, masked):
        kf = k_ref[pl.ds(koff, width), :]
        vb = v_ref[pl.ds(koff, width), :].astype(BF16)
        lo_half = lax.broadcasted_iota(jnp.int32, (width, LANES), 1) < HEAD_DIM
        kxs = (jnp.where(lo_half, kf, 0.0).astype(BF16),
               jnp.where(lo_half, 0.0, kf).astype(BF16))
        q8 = q8_ref[...]
        if masked:
            qi = lax.broadcasted_iota(jnp.int32, (rows, width), 0) & (TQ - 1)
            ki = lax.broadcasted_iota(jnp.int32, (rows, width), 1)
            causal = ki <= qi
        for mi in range(2):
            s = lax.dot_general(q8, kxs[mi], _NT, preferred_element_type=F32)
            if masked:
                s = jnp.where(causal, s, NEG)
            m_old = m_ref[mi]
            m_new = jnp.maximum(m_old, jnp.max(s, axis=-1, keepdims=True))
            alpha = jnp.exp2(m_old - m_new)
            p = jnp.exp2(s - _rep(m_new, width // LANES))
            l_ref[mi] = alpha * l_ref[mi] + jnp.sum(p, axis=-1, keepdims=True)
            acc_ref[mi] = alpha * acc_ref[mi] + jnp.dot(
                p.astype(BF16), vb, preferred_element_type=F32)
            m_ref[mi] = m_new

    n_bulk = (qt * TQ) // TKB

    def bulk(i, carry):
        tile(pl.multiple_of(i * TKB, TKB), TKB, False)
        return carry

    def small(i, carry):
        tile(pl.multiple_of(i * TQ, TQ), TQ, False)
        return carry

    def bulk2(j, carry):
        tile(pl.multiple_of(2 * j * TKB, TKB), TKB, False)
        tile(pl.multiple_of((2 * j + 1) * TKB, TKB), TKB, False)
        return carry

    lax.fori_loop(0, n_bulk // 2, bulk2, 0)
    lax.fori_loop((n_bulk // 2) * 2, n_bulk, bulk, 0)
    lax.fori_loop(n_bulk * (TKB // TQ), qt, small, 0)
    tile(pl.multiple_of(qt * TQ, TQ), TQ, True)

    lam = _lambda(lq1_ref, lk1_ref, lq2_ref, lk2_ref, lam_init)
    o = acc_ref[0] * (1.0 / l_ref[0]) - lam * (acc_ref[1] * (1.0 / l_ref[1]))
    on = _subln(o, sub_ref[...], lam_init)
    for g in range(GS):
        z = z_ref[:, g * LANES:(g + 1) * LANES]
        o_ref[:, g * LANES:(g + 1) * LANES] = (on[g * TQ:(g + 1) * TQ] * _silu(z)).astype(BF16)


def _diff_prompt(q, k, h, lam_vecs, subln, lam_init, seq):
    gw = GS * VD_B
    n_hg = G_B // GS
    vec = pl.BlockSpec((1, HEAD_DIM), lambda c, g, t: (0, 0))
    return pl.pallas_call(
        functools.partial(_diff_prompt_body, lam_init=lam_init),
        grid=(KV_B, n_hg, seq // TQ),
        in_specs=[
            vec, vec, vec, vec,
            pl.BlockSpec((1, VD_B), lambda c, g, t: (0, 0)),
            pl.BlockSpec((TQ, gw), lambda c, g, t: (t, c * n_hg + g)),
            pl.BlockSpec((seq, VD_B), lambda c, g, t: (0, c)),
            pl.BlockSpec((seq, VD_B), lambda c, g, t: (0, COL_V // VD_B + c)),
            pl.BlockSpec((TQ, gw), lambda c, g, t: (t, COL_Z // gw + c * n_hg + g)),
        ],
        out_specs=pl.BlockSpec((TQ, gw), lambda c, g, t: (t, c * n_hg + g)),
        out_shape=jax.ShapeDtypeStruct((seq, W_Q), BF16),
        scratch_shapes=[
            pltpu.VMEM((GS * TQ, LANES), BF16),
            pltpu.VMEM((2, GS * TQ, LANES), F32),
            pltpu.VMEM((2, GS * TQ, LANES), F32),
            pltpu.VMEM((2, GS * TQ, VD_B), F32),
        ],
        compiler_params=_cparams(3),
        name="diff_prompt",
    )(*lam_vecs, subln.reshape(1, VD_B), q, k, h, h)


def _diff_sample_body(pt_ref, lq1_ref, lk1_ref, lq2_ref, lk2_ref, sub_ref,
                      qbd_ref, kn_ref, vn_ref, *rest, lam_init, dec_seq, n_chunks):
    kp_refs = rest[:NP]
    vp_refs = rest[NP:2 * NP]
    o_ref, m_ref, l_ref, acc_ref = rest[2 * NP:]
    c = pl.program_id(1)
    rows = dec_seq * G_B

    @pl.when(c == 0)
    def _():
        m_ref[...] = jnp.full(m_ref.shape, NEG, F32)
        l_ref[...] = jnp.zeros(l_ref.shape, F32)
        acc_ref[...] = jnp.zeros(acc_ref.shape, F32)

    nr = 2 * rows

    def update(ss, vbs):
        s = jnp.concatenate(ss, axis=0)
        m_old = m_ref[...]
        m_new = jnp.maximum(m_old, jnp.max(s, axis=-1, keepdims=True))
        alpha = jnp.exp2(m_old - m_new)
        pb = jnp.exp2(s - _rep(m_new, s.shape[1] // LANES)).astype(BF16)
        ones = jnp.ones((s.shape[1], LANES), BF16)
        pv = jnp.concatenate(
            [jnp.dot(pb[kv * nr:(kv + 1) * nr], jnp.concatenate([vbs[kv], ones], axis=1),
                     preferred_element_type=F32)
             for kv in range(KV_B)], axis=0)
        l_ref[...] = alpha * l_ref[...] + pv[:, VD_B:]
        acc_ref[...] = alpha * acc_ref[...] + pv[:, :VD_B]
        m_ref[...] = m_new

    ss, vbs = [], []
    for kv in range(KV_B):
        rs = pl.ds(kv, PAGE, stride=KV_B)
        kb = jnp.concatenate([r[rs, :] for r in kp_refs], axis=0).astype(BF16)
        vbs.append(jnp.concatenate([r[rs, :] for r in vp_refs], axis=0).astype(BF16))
        ss.append(lax.dot_general(qbd_ref[kv], kb, _NT, preferred_element_type=F32))
    update(ss, vbs)

    @pl.when(c == n_chunks - 1)
    def _():
        lam = _lambda(lq1_ref, lk1_ref, lq2_ref, lk2_ref, lam_init)
        r = lax.broadcasted_iota(jnp.int32, (nr, LANES), 0)
        t = (r & (rows - 1)) >> (G_B.bit_length() - 1)
        i = lax.broadcasted_iota(jnp.int32, (nr, LANES), 1)
        valid = i <= t
        zpad = jnp.zeros((LANES - 8, LANES), F32)
        ss, vbs = [], []
        for kv in range(KV_B):
            cs = slice(kv * LANES, (kv + 1) * LANES)
            kb = jnp.concatenate([kn_ref[:, cs], zpad], axis=0).astype(BF16)
            vbs.append(jnp.concatenate([vn_ref[:, cs], zpad], axis=0).astype(BF16))
            s = lax.dot_general(qbd_ref[kv], kb, _NT, preferred_element_type=F32)
            ss.append(jnp.where(valid, s, NEG))
        update(ss, vbs)
        on = acc_ref[...] * (1.0 / l_ref[...])
        for kv in range(KV_B):
            o = on[kv * nr:kv * nr + rows] - lam * on[kv * nr + rows:(kv + 1) * nr]
            o_ref[kv] = _subln(o, sub_ref[...], lam_init)


def _diff_sample(pt_flat, qbd, kn8, vn8, cache_k, cache_v, lam_vecs, subln, lam_init,
                 dec_batch, dec_seq, n_pages):
    rows = dec_seq * G_B
    n_chunks = n_pages // NP
    wkv = KV_B * VD_B
    vec = pl.BlockSpec((1, HEAD_DIM), lambda b, c, pt: (0, 0))

    def page_spec(i):
        return pl.BlockSpec((None, PAGE * KV_B, VD_B),
                            lambda b, c, pt: (pt[b * n_pages + c * NP + i], 0, 0))

    grid_spec = pltpu.PrefetchScalarGridSpec(
        num_scalar_prefetch=1,
        grid=(dec_batch, n_chunks),
        in_specs=[
            vec, vec, vec, vec,
            pl.BlockSpec((1, VD_B), lambda b, c, pt: (0, 0)),
            pl.BlockSpec((None, KV_B, 2 * rows, LANES), lambda b, c, pt: (b, 0, 0, 0)),
            pl.BlockSpec((None, 8, wkv), lambda b, c, pt: (b, 0, 0)),
            pl.BlockSpec((None, 8, wkv), lambda b, c, pt: (b, 0, 0)),
        ] + [page_spec(i) for i in range(NP)] + [page_spec(i) for i in range(NP)],
        out_specs=pl.BlockSpec((None, KV_B, rows, VD_B), lambda b, c, pt: (b, 0, 0, 0)),
        scratch_shapes=[
            pltpu.VMEM((KV_B * 2 * rows, LANES), F32),
            pltpu.VMEM((KV_B * 2 * rows, LANES), F32),
            pltpu.VMEM((KV_B * 2 * rows, VD_B), F32),
        ],
    )
    return pl.pallas_call(
        functools.partial(_diff_sample_body, lam_init=lam_init, dec_seq=dec_seq,
                          n_chunks=n_chunks),
        grid_spec=grid_spec,
        out_shape=jax.ShapeDtypeStruct((dec_batch, KV_B, rows, VD_B), F32),
        compiler_params=_cparams(2),
        name="diff_sample",
    )(pt_flat, *lam_vecs, subln.reshape(1, VD_B), qbd, kn8, vn8,
      *([cache_k] * NP), *([cache_v] * NP))


def _gate_body(o_ref, z_ref, g_ref):
    g_ref[...] = (o_ref[...] * _silu(z_ref[...])).astype(BF16)


def _gate_sample(o_s, h, row_blk):
    n = o_s.shape[0]
    return pl.pallas_call(
        _gate_body,
        grid=(W_Z // 1024,),
        in_specs=[
            pl.BlockSpec((n, 1024), lambda c: (0, c)),
            pl.BlockSpec((n, 1024), lambda c: (row_blk, COL_Z // 1024 + c)),
        ],
        out_specs=pl.BlockSpec((n, 1024), lambda c: (0, c)),
        out_shape=jax.ShapeDtypeStruct((n, W_Z), BF16),
        compiler_params=_cparams(1),
        name="gate_sample",
    )(o_s, h)


def _outproj_body(gp_ref, gs_ref, w_ref, xp_ref, xs_ref, yp_ref, ys_ref, wb_ref,
                  *, n_prompt_tiles):
    i = pl.program_id(1)

    @pl.when(i == 0)
    def _():
        wb_ref[...] = w_ref[...].astype(BF16)

    @pl.when(i < n_prompt_tiles)
    def _():
        yp_ref[...] = xp_ref[...] + jnp.dot(gp_ref[...], wb_ref[...], preferred_element_type=F32)

    @pl.when(i >= n_prompt_tiles)
    def _():
        ys_ref[...] = xs_ref[...] + jnp.dot(gs_ref[...], wb_ref[...], preferred_element_type=F32)


def _outproj(gp, gs, w, xp, xs):
    seq, n_dec = gp.shape[0], gs.shape[0]
    assert n_dec == TM and seq % TM == 0
    n_prompt_tiles = seq // TM
    last = n_prompt_tiles - 1
    return pl.pallas_call(
        functools.partial(_outproj_body, n_prompt_tiles=n_prompt_tiles),
        grid=(D_MODEL // TN_OUT, n_prompt_tiles + 1),
        in_specs=[
            pl.BlockSpec((TM, W_Z), lambda j, i: (jnp.minimum(i, last), 0)),
            pl.BlockSpec((TM, W_Z), lambda j, i: (0, 0)),
            pl.BlockSpec((W_Z, TN_OUT), lambda j, i: (0, j)),
            pl.BlockSpec((TM, TN_OUT), lambda j, i: (jnp.minimum(i, last), j)),
            pl.BlockSpec((TM, TN_OUT), lambda j, i: (0, j)),
        ],
        out_specs=[
            pl.BlockSpec((TM, TN_OUT), lambda j, i: (jnp.minimum(i, last), j)),
            pl.BlockSpec((TM, TN_OUT), lambda j, i: (0, j)),
        ],
        out_shape=[
            jax.ShapeDtypeStruct((seq, D_MODEL), F32),
            jax.ShapeDtypeStruct((n_dec, D_MODEL), F32),
        ],
        scratch_shapes=[pltpu.VMEM((W_Z, TN_OUT), BF16)],
        compiler_params=_cparams(2),
        name="outproj",
    )(gp, gs, w, xp, xs)


def _rope_tables(seq, dec_batch, dec_seq, past_len):
    half = HEAD_DIM // 2
    inv_freq = 1.0 / (ROPE_THETA ** (jnp.arange(0, HEAD_DIM, 2, dtype=F32) / HEAD_DIM))
    pos = jnp.concatenate([
        jnp.arange(seq, dtype=jnp.int32),
        jnp.tile(past_len + jnp.arange(dec_seq, dtype=jnp.int32), dec_batch),
    ])
    ang = pos.astype(F32)[:, None] * inv_freq[None, :]
    cos = jnp.cos(ang)
    sin = jnp.sin(ang)
    reps = LANES // HEAD_DIM
    cos_t = jnp.concatenate([cos, cos] * reps, axis=1)
    sin_t = jnp.concatenate([-sin, sin] * reps, axis=1)
    assert cos_t.shape[1] == LANES and half * 2 == HEAD_DIM
    return cos_t, sin_t


def kernel(x_prompt, x_sample, cache_win_k, cache_win_v, cache_k, cache_v, page_table,
           norm_a, w_in_a, q_norm_a, k_norm_a, sinks_a, w_out_a,
           norm_b, w_in_b, q_norm_b, k_norm_b, lambda_q1, lambda_k1, lambda_q2, lambda_k2,
           subln_b, w_out_b):
    batch, seq, _ = x_prompt.shape
    dec_batch, dec_seq, _ = x_sample.shape
    n_pool = cache_k.shape[1]
    n_pages = page_table.shape[1]
    past_len = n_pages * cache_k.shape[2]
    n_dec = dec_batch * dec_seq
    n_rows = seq + n_dec
    assert batch == 1 and n_dec == TM and seq % TM == 0 and cache_k.shape[2] == PAGE
    assert cache_win_k.shape[2] == WINDOW and n_pages % NP == 0
    s_blk = seq // TM

    cos_t, sin_t = _rope_tables(seq, dec_batch, dec_seq, past_len)
    xp = x_prompt.reshape(seq, D_MODEL)
    xs = x_sample.reshape(n_dec, D_MODEL)

    h = _inproj(xp, xs, norm_a[0], w_in_a[0].astype(BF16))
    q, k = _qkpost(h, cos_t, sin_t, q_norm_a[0], k_norm_a[0], n_rows, paged=False)
    v = h[:, COL_V:COL_V + W_V]

    gp = _win_prompt(q, k, h, sinks_a[0], seq)

    rows_a = dec_seq * G_A
    qs = q[seq:].reshape(dec_batch, dec_seq, KV_A, G_A, HEAD_DIM)
    qs = qs.transpose(0, 2, 1, 3, 4).reshape(dec_batch, KV_A, rows_a, HEAD_DIM)
    qd = jnp.concatenate([qs, qs], axis=-1)
    sink_tab = jnp.broadcast_to(sinks_a[0].astype(F32).reshape(KV_A, 1, G_A, 1),
                                (KV_A, dec_seq, G_A, LANES)).reshape(KV_A * rows_a, LANES)
    wkv_a = KV_A * HEAD_DIM
    kn = k[seq:].reshape(dec_batch, dec_seq, wkv_a)
    vn = v[seq:].reshape(dec_batch, dec_seq, wkv_a)
    o_s, win_k_s, win_v_s = _win_sample(
        qd, sink_tab,
        cache_win_k[0].transpose(0, 2, 3, 1).reshape(dec_batch, wkv_a, WINDOW),
        cache_win_v[0].transpose(0, 2, 3, 1).reshape(dec_batch, wkv_a, WINDOW),
        kn, vn, dec_batch, dec_seq)
    o_s = o_s.reshape(dec_batch, KV_A // 2, dec_seq, G_A, 2, HEAD_DIM)
    o_s = o_s.transpose(0, 2, 1, 4, 3, 5).reshape(n_dec, W_Q)
    gs = _gate_sample(o_s, h, s_blk)

    x1p, x1s = _outproj(gp, gs, w_out_a[0], xp, xs)

    wl = min(WINDOW, seq)
    win_k_p = k[seq - wl:seq].reshape(1, batch, wl, KV_A, HEAD_DIM)
    win_v_p = v[seq - wl:seq].reshape(1, batch, wl, KV_A, HEAD_DIM)
    win_k_s = win_k_s.reshape(dec_batch, KV_A, HEAD_DIM, WINDOW).transpose(0, 3, 1, 2)[None]
    win_v_s = win_v_s.reshape(dec_batch, KV_A, HEAD_DIM, WINDOW).transpose(0, 3, 1, 2)[None]

    lam_init = 0.8 - 0.6 * math.exp(-0.3 * 1)
    lam_vecs = [a[0].astype(F32).reshape(1, HEAD_DIM)
                for a in (lambda_q1, lambda_k1, lambda_q2, lambda_k2)]

    h = _inproj(x1p, x1s, norm_b[0], w_in_b[0].astype(BF16))
    q, k, k_pg, v_pg = _qkpost(h, cos_t, sin_t, q_norm_b[0], k_norm_b[0], n_rows, paged=True)
    v = h[:, COL_V:COL_V + W_V]

    gp = _diff_prompt(q, k, h, lam_vecs, subln_b[0].astype(F32), lam_init, seq)

    rows_b = dec_seq * G_B
    qs = q[seq:].reshape(dec_batch, dec_seq, KV_B, G_B, 2, HEAD_DIM)
    qs = qs.transpose(0, 2, 4, 1, 3, 5).reshape(dec_batch, KV_B, 2, rows_b, HEAD_DIM)
    zq = jnp.zeros_like(qs[:, :, 0])
    qbd = jnp.concatenate([
        jnp.concatenate([qs[:, :, 0], zq], axis=-1),
        jnp.concatenate([zq, qs[:, :, 1]], axis=-1),
    ], axis=2)
    wkv_b = KV_B * VD_B
    kn = k[seq:].reshape(dec_batch, dec_seq, wkv_b)
    vn = v[seq:].reshape(dec_batch, dec_seq, wkv_b)
    padn = jnp.zeros((dec_batch, 8 - dec_seq, wkv_b), F32)
    kn8 = jnp.concatenate([kn, padn], axis=1)
    vn8 = jnp.concatenate([vn, padn], axis=1)
    o_s = _diff_sample(
        page_table.reshape(-1).astype(jnp.int32), qbd, kn8, vn8,
        cache_k[0].reshape(n_pool, PAGE * KV_B, VD_B), cache_v[0].reshape(n_pool, PAGE * KV_B, VD_B),
        lam_vecs, subln_b[0].astype(F32), lam_init, dec_batch, dec_seq, n_pages)
    o_s = o_s.reshape(dec_batch, KV_B, dec_seq, G_B * VD_B)
    o_s = o_s.transpose(0, 2, 1, 3).reshape(n_dec, W_Q)
    gs = _gate_sample(o_s, h, s_blk)

    x2p, x2s = _outproj(gp, gs, w_out_b[0], x1p, x1s)

    y_prompt = x2p.reshape(batch, seq, D_MODEL)
    y_sample = x2s.reshape(dec_batch, dec_seq, D_MODEL)
    kv_k_p = k_pg[:seq * KV_B].reshape(1, batch, seq // PAGE, PAGE, KV_B, 2 * HEAD_DIM)
    kv_v_p = v_pg[:seq * KV_B].reshape(1, batch, seq // PAGE, PAGE, KV_B, VD_B)
    kv_k_s = k_pg[seq * KV_B:].reshape(1, dec_batch, dec_seq, KV_B, 2 * HEAD_DIM)
    kv_v_s = v_pg[seq * KV_B:].reshape(1, dec_batch, dec_seq, KV_B, VD_B)

    return (y_prompt, y_sample, win_k_p, win_v_p, win_k_s, win_v_s,
            kv_k_p, kv_v_p, kv_k_s, kv_v_s)
```

```python
import functools
import math

import jax
import jax.numpy as jnp
from jax import lax
from jax.experimental import pallas as pl
from jax.experimental.pallas import tpu as pltpu

F32 = jnp.float32
BF16 = jnp.bfloat16

D_MODEL = 2048
HEAD_DIM = 64
ROPE_THETA = 10000.0
EPS = 1e-6
WINDOW = 128
KV_A = 8
G_A = 8
KV_B = 4
G_B = 8
VD_B = 128
PAGE = 128
NEG = -1e30

W_Q = 4096
W_K = 512
W_V = 512
W_Z = 4096
W_IN = W_Q + W_K + W_V + W_Z
COL_K = W_Q
COL_V = W_Q + W_K
COL_Z = W_Q + W_K + W_V

LANES = 128
VMEM_LIMIT = 56 * 1024 * 1024

TM = 512
TN_IN = 2304
TN_OUT = 512
TQ = 256
TKB = 1024
GS = 8
NP = 32

LOG2E = math.log2(math.e)
Q_SCALE = HEAD_DIM ** -0.5 * LOG2E

_NT = (((1,), (1,)), ((), ()))


def _cparams(n_axes, flags=None):
    return pltpu.CompilerParams(
        dimension_semantics=("arbitrary",) * n_axes,
        vmem_limit_bytes=VMEM_LIMIT,
        flags=flags,
    )


def _silu(z):
    return z * (1.0 / (1.0 + jnp.exp(-z)))


def _inproj_body(xp_ref, xs_ref, g_ref, w_ref, h_ref, xn_ref, *, n_prompt_tiles):
    i = pl.program_id(0)
    j = pl.program_id(1)

    def norm(x_ref):
        x = x_ref[...]
        ms = jnp.mean(x * x, axis=-1, keepdims=True)
        xn_ref[...] = (x * lax.rsqrt(ms + EPS) * g_ref[...]).astype(BF16)

    @pl.when((j == 0) & (i < n_prompt_tiles))
    def _():
        norm(xp_ref)

    @pl.when((j == 0) & (i >= n_prompt_tiles))
    def _():
        norm(xs_ref)

    h_ref[...] = jnp.dot(xn_ref[...], w_ref[...], preferred_element_type=F32)


def _inproj(xp, xs, gain, w_bf16):
    seq, n_dec = xp.shape[0], xs.shape[0]
    assert n_dec == TM and seq % TM == 0
    n_prompt_tiles = seq // TM
    n_rows = seq + n_dec
    grid = (n_rows // TM, W_IN // TN_IN)
    return pl.pallas_call(
        functools.partial(_inproj_body, n_prompt_tiles=n_prompt_tiles),
        grid=grid,
        in_specs=[
            pl.BlockSpec((TM, D_MODEL), lambda i, j: (jnp.minimum(i, n_prompt_tiles - 1), 0)),
            pl.BlockSpec((TM, D_MODEL), lambda i, j: (0, 0)),
            pl.BlockSpec((1, D_MODEL), lambda i, j: (0, 0)),
            pl.BlockSpec((D_MODEL, TN_IN), lambda i, j: (0, j)),
        ],
        out_specs=pl.BlockSpec((TM, TN_IN), lambda i, j: (i, j)),
        out_shape=jax.ShapeDtypeStruct((n_rows, W_IN), F32),
        scratch_shapes=[pltpu.VMEM((TM, D_MODEL), BF16)],
        compiler_params=_cparams(2),
        name="inproj",
    )(xp, xs, gain.reshape(1, D_MODEL), w_bf16)


TR = 256
CW = 256


def _qkpost_body(h_ref, cos_ref, sin_ref, qg_ref, kg_ref, bd_ref, q_ref, k_ref, *paged_refs,
                 n_prompt_tiles):
    cos = jnp.concatenate([cos_ref[...]] * (CW // LANES), axis=1)
    sin = jnp.concatenate([sin_ref[...]] * (CW // LANES), axis=1)
    bd = bd_ref[...]
    lane = lax.broadcasted_iota(jnp.int32, (TR, CW), 1)
    first_half = (lane & (HEAD_DIM - 1)) < (HEAD_DIM // 2)

    def normrope(c, gain):
        t = h_ref[:, c * CW:(c + 1) * CW]
        ss = jnp.dot((t * t).astype(BF16), bd, preferred_element_type=F32)
        tn = t * lax.rsqrt(ss * (1.0 / HEAD_DIM) + EPS) * gain
        partner = jnp.where(first_half,
                            pltpu.roll(tn, CW - HEAD_DIM // 2, 1),
                            pltpu.roll(tn, HEAD_DIM // 2, 1))
        return tn * cos + partner * sin

    qg = qg_ref[...]
    kg = kg_ref[...]
    for c in range(W_Q // CW):
        q_ref[:, c * CW:(c + 1) * CW] = (normrope(c, qg) * Q_SCALE).astype(BF16)
    for c in range(W_K // CW):
        k_ref[:, c * CW:(c + 1) * CW] = normrope(W_Q // CW + c, kg)
    if paged_refs:
        def write_paged(k_dst, v_dst):
            for u in range(KV_B):
                rs = pl.ds(u, TR, stride=KV_B)
                k_dst[rs, :] = k_ref[:, u * VD_B:(u + 1) * VD_B]
                v_dst[rs, :] = h_ref[:, COL_V + u * VD_B:COL_V + (u + 1) * VD_B]

        is_prompt = pl.program_id(0) < n_prompt_tiles
        pl.when(is_prompt)(functools.partial(write_paged, paged_refs[0], paged_refs[1]))
        pl.when(jnp.logical_not(is_prompt))(
            functools.partial(write_paged, paged_refs[2], paged_refs[3]))


def _qkpost(h, cos, sin, q_gain, k_gain, seq, n_rows, paged):
    head = jnp.arange(CW, dtype=jnp.int32) // HEAD_DIM
    bd = (head[:, None] == head[None, :]).astype(BF16)
    qg = jnp.tile(q_gain.astype(F32), CW // HEAD_DIM).reshape(1, CW)
    kg = jnp.tile(k_gain.astype(F32), CW // HEAD_DIM).reshape(1, CW)
    in_w = W_Q + W_K + (W_V if paged else 0)
    assert seq % TR == 0 and (n_rows - seq) % TR == 0
    n_pt = seq // TR
    paged_specs, paged_shapes = [], []
    if paged:
        paged_specs = (
            [pl.BlockSpec((TR * KV_B, VD_B), lambda i: (jnp.minimum(i, n_pt - 1), 0))] * 2
            + [pl.BlockSpec((TR * KV_B, VD_B), lambda i: (jnp.maximum(i - n_pt, 0), 0))] * 2)
        paged_shapes = ([jax.ShapeDtypeStruct((seq * KV_B, VD_B), F32)] * 2
                        + [jax.ShapeDtypeStruct(((n_rows - seq) * KV_B, VD_B), F32)] * 2)
    return pl.pallas_call(
        functools.partial(_qkpost_body, n_prompt_tiles=n_pt),
        grid=(n_rows // TR,),
        in_specs=[
            pl.BlockSpec((TR, in_w), lambda i: (i, 0)),
            pl.BlockSpec((TR, LANES), lambda i: (i, 0)),
            pl.BlockSpec((TR, LANES), lambda i: (i, 0)),
            pl.BlockSpec((1, CW), lambda i: (0, 0)),
            pl.BlockSpec((1, CW), lambda i: (0, 0)),
            pl.BlockSpec((CW, CW), lambda i: (0, 0)),
        ],
        out_specs=[
            pl.BlockSpec((TR, W_Q), lambda i: (i, 0)),
            pl.BlockSpec((TR, W_K), lambda i: (i, 0)),
        ] + paged_specs,
        out_shape=[
            jax.ShapeDtypeStruct((n_rows, W_Q), BF16),
            jax.ShapeDtypeStruct((n_rows, W_K), F32),
        ] + paged_shapes,
        compiler_params=_cparams(1),
        name="qkpost",
    )(h, cos, sin, qg, kg, bd)


def _half_split(x, lo_half, even):
    if even:
        a = jnp.where(lo_half, x, 0.0)
        return a, pltpu.roll(a, HEAD_DIM, 1)
    b = jnp.where(lo_half, 0.0, x)
    return pltpu.roll(b, HEAD_DIM, 1), b


def _win_prompt_body(sink_ref, q_ref, kp_ref, kc_ref, vp_ref, vc_ref,
                     z0_ref, z1_ref, z2_ref, z3_ref, o_ref):
    n = pl.program_id(0)
    z_refs = (z0_ref, z1_ref, z2_ref, z3_ref)
    nk = 2 * WINDOW
    lo_half = lax.broadcasted_iota(jnp.int32, (nk, LANES), 1) < HEAD_DIM
    rows = 4 * WINDOW
    qi = lax.broadcasted_iota(jnp.int32, (rows, nk), 0) & (WINDOW - 1)
    ki = lax.broadcasted_iota(jnp.int32, (rows, nk), 1)
    valid = (ki > qi) & (ki <= qi + WINDOW) & (ki >= jnp.where(n > 0, 0, WINDOW))

    ones_k = jnp.ones((nk, LANES), BF16)

    def softmax(s, sink):
        s = jnp.where(valid, s, NEG)
        m = jnp.maximum(jnp.broadcast_to(jnp.max(s, axis=-1, keepdims=True), sink.shape), sink)
        p = jnp.exp2(s - _rep(m, nk // LANES)).astype(BF16)
        l = jnp.dot(p, ones_k, preferred_element_type=F32) + jnp.exp2(sink - m)
        return p, 1.0 / l

    for pr in range(KV_A // 2):
        cs = slice(pr * LANES, (pr + 1) * LANES)
        kk = jnp.concatenate([kp_ref[:, cs], kc_ref[:, cs]], axis=0)
        vv = jnp.concatenate([vp_ref[:, cs], vc_ref[:, cs]], axis=0)
        for sub in range(2):
            kv = 2 * pr + sub
            ka, kb = _half_split(kk, lo_half, sub == 0)
            va, vb = _half_split(vv, lo_half, sub == 0)
            base = kv * G_A * HEAD_DIM
            q4 = jnp.concatenate(
                [q_ref[:, base + j * LANES: base + (j + 1) * LANES] for j in range(4)], axis=0)
            s_e = lax.dot_general(q4, ka.astype(BF16), _NT, preferred_element_type=F32)
            s_o = lax.dot_general(q4, kb.astype(BF16), _NT, preferred_element_type=F32)
            sink_e = jnp.concatenate(
                [jnp.full((WINDOW, LANES), sink_ref[kv * G_A + 2 * j] * LOG2E, F32)
                 for j in range(4)], axis=0)
            sink_o = jnp.concatenate(
                [jnp.full((WINDOW, LANES), sink_ref[kv * G_A + 2 * j + 1] * LOG2E, F32)
                 for j in range(4)], axis=0)
            p_e, r_e = softmax(s_e, sink_e)
            p_o, r_o = softmax(s_o, sink_o)
            o = (jnp.dot(p_e, va.astype(BF16), preferred_element_type=F32) * r_e
                 + jnp.dot(p_o, vb.astype(BF16), preferred_element_type=F32) * r_o)
            zr = z_refs[kv // 2]
            zb = (kv % 2) * 512
            for j in range(4):
                z = zr[:, zb + j * LANES: zb + (j + 1) * LANES]
                o_ref[:, base + j * LANES: base + (j + 1) * LANES] = (
                    o[j * WINDOW:(j + 1) * WINDOW] * _silu(z)).astype(BF16)


def _win_prompt(q, k, h, sinks, seq):
    nb = seq // WINDOW
    prev = lambda n: (jnp.maximum(n - 1, 0), 0)
    zspec = lambda c: pl.BlockSpec((WINDOW, 1024), lambda n: (n, COL_Z // 1024 + c))
    return pl.pallas_call(
        _win_prompt_body,
        grid=(nb,),
        in_specs=[
            pl.BlockSpec(memory_space=pltpu.SMEM),
            pl.BlockSpec((WINDOW, W_Q), lambda n: (n, 0)),
            pl.BlockSpec((WINDOW, W_K), prev),
            pl.BlockSpec((WINDOW, W_K), lambda n: (n, 0)),
            pl.BlockSpec((WINDOW, W_V), lambda n: (jnp.maximum(n - 1, 0), COL_V // W_V)),
            pl.BlockSpec((WINDOW, W_V), lambda n: (n, COL_V // W_V)),
            zspec(0), zspec(1), zspec(2), zspec(3),
        ],
        out_specs=pl.BlockSpec((WINDOW, W_Q), lambda n: (n, 0)),
        out_shape=jax.ShapeDtypeStruct((seq, W_Q), BF16),
        compiler_params=_cparams(1),
        name="win_prompt",
    )(sinks.astype(F32), q, k, k, h, h, h, h, h, h)


def _win_sample_body(q_ref, sink_ref, ck_ref, cv_ref, kn_ref, vn_ref,
                     o_ref, ok_ref, ov_ref, kpad_ref, vpad_ref, *, dec_seq):
    keep = WINDOW - dec_seq
    wkv = KV_A * HEAD_DIM
    lane = lax.broadcasted_iota(jnp.int32, (wkv, WINDOW), 1)

    for pad, new in ((kpad_ref, kn_ref), (vpad_ref, vn_ref)):
        pad[...] = jnp.zeros(pad.shape, F32)
        pad[0:dec_seq, :] = new[...]
    knt = kpad_ref[...].T
    vnt = vpad_ref[...].T

    ok_ref[...] = jnp.where(lane >= keep, pltpu.roll(knt, keep, 1), pltpu.roll(ck_ref[...], keep, 1))
    ov_ref[...] = jnp.where(lane >= keep, pltpu.roll(vnt, keep, 1), pltpu.roll(cv_ref[...], keep, 1))

    rows = dec_seq * G_A
    all_rows = KV_A * rows
    t = (lax.broadcasted_iota(jnp.int32, (all_rows, WINDOW), 0) & (rows - 1)) >> (
        G_A.bit_length() - 1)
    i = lax.broadcasted_iota(jnp.int32, (all_rows, WINDOW), 1)
    valid_old = i > t
    valid_new = i <= t

    s_old, s_new = [], []
    for kv in range(KV_A):
        hs = slice(kv * HEAD_DIM, (kv + 1) * HEAD_DIM)
        q = q_ref[kv]
        s_old.append(jnp.dot(q, ck_ref[hs, :].astype(BF16), preferred_element_type=F32))
        s_new.append(jnp.dot(q, knt[hs, :].astype(BF16), preferred_element_type=F32))
    s = jnp.concatenate([jnp.where(valid_old, jnp.concatenate(s_old, axis=0), NEG),
                         jnp.where(valid_new, jnp.concatenate(s_new, axis=0), NEG)], axis=1)
    sink = sink_ref[...] * LOG2E
    m = jnp.maximum(jnp.broadcast_to(jnp.max(s, axis=-1, keepdims=True), sink.shape), sink)
    p = jnp.exp2(s - _rep(m, 2)).astype(BF16)
    l = jnp.dot(p, jnp.ones((2 * WINDOW, LANES), BF16), preferred_element_type=F32) + jnp.exp2(
        sink - m)
    r = 1.0 / l
    for kv in range(KV_A):
        hs = slice(kv * HEAD_DIM, (kv + 1) * HEAD_DIM)
        rs = slice(kv * rows, (kv + 1) * rows)
        pv = (lax.dot_general(p[rs, :WINDOW], cv_ref[hs, :].astype(BF16), _NT,
                              preferred_element_type=F32)
              + lax.dot_general(p[rs, WINDOW:], vnt[hs, :].astype(BF16), _NT,
                                preferred_element_type=F32))
        o_ref[kv] = pv * r[rs, :HEAD_DIM]


def _win_sample(q, sink_tab, ck, cv, kn, vn, dec_batch, dec_seq):
    rows = dec_seq * G_A
    wkv = KV_A * HEAD_DIM
    bspec = lambda shp: pl.BlockSpec((None,) + shp, lambda b: (b,) + (0,) * len(shp))
    return pl.pallas_call(
        functools.partial(_win_sample_body, dec_seq=dec_seq),
        grid=(dec_batch,),
        in_specs=[
            bspec((KV_A, rows, HEAD_DIM)),
            pl.BlockSpec((KV_A * rows, LANES), lambda b: (0, 0)),
            bspec((wkv, WINDOW)), bspec((wkv, WINDOW)),
            bspec((dec_seq, wkv)), bspec((dec_seq, wkv)),
        ],
        out_specs=[
            bspec((KV_A, rows, HEAD_DIM)),
            bspec((wkv, WINDOW)), bspec((wkv, WINDOW)),
        ],
        out_shape=[
            jax.ShapeDtypeStruct((dec_batch, KV_A, rows, HEAD_DIM), F32),
            jax.ShapeDtypeStruct((dec_batch, wkv, WINDOW), F32),
            jax.ShapeDtypeStruct((dec_batch, wkv, WINDOW), F32),
        ],
        scratch_shapes=[pltpu.VMEM((LANES, wkv), F32), pltpu.VMEM((LANES, wkv), F32)],
        compiler_params=_cparams(1),
        name="win_sample",
    )(q, sink_tab, ck, cv, kn, vn)


def _lambda(lq1_ref, lk1_ref, lq2_ref, lk2_ref, lam_init):
    a = jnp.sum(lq1_ref[...] * lk1_ref[...], axis=-1, keepdims=True)
    b = jnp.sum(lq2_ref[...] * lk2_ref[...], axis=-1, keepdims=True)
    return jnp.exp(a) - jnp.exp(b) + lam_init


def _subln(o, gain, lam_init):
    ms = jnp.mean(o * o, axis=-1, keepdims=True)
    return o * lax.rsqrt(ms + EPS) * gain * (1.0 - lam_init)


def _rep(x, n):
    return jnp.concatenate([x] * n, axis=1) if n > 1 else x


def _diff_prompt_body(lq1_ref, lk1_ref, lq2_ref, lk2_ref, sub_ref,
                      q_ref, k_ref, v_ref, z_ref, o_ref,
                      q8_ref, m_ref, l_ref, acc_ref, *, lam_init):
    qt = pl.program_id(2)
    rows = GS * TQ
    for g in range(GS):
        q8_ref[g * TQ:(g + 1) * TQ, :] = q_ref[:, g * LANES:(g + 1) * LANES]
    m_ref[...] = jnp.full(m_ref.shape, NEG, F32)
    l_ref[...] = jnp.zeros(l_ref.shape, F32)
    acc_ref[...] = jnp.zeros(acc_ref.shape, F32)

    def tile(koff, width, masked):
        kf = k_ref[pl.ds(koff, width), :]
        vb = v_ref[pl.ds(koff, width), :].astype(BF16)
        lo_half = lax.broadcasted_iota(jnp.int32, (width, LANES), 1) < HEAD_DIM
        kxs = (jnp.where(lo_half, kf, 0.0).astype(BF16),
               jnp.where(lo_half, 0.0, kf).astype(BF16))
        q8 = q8_ref[...]
        if masked:
            qi = lax.broadcasted_iota(jnp.int32, (rows, width), 0) & (TQ - 1)
            ki = lax.broadcasted_iota(jnp.int32, (rows, width), 1)
            causal = ki <= qi
        for mi in range(2):
            s = lax.dot_general(q8, kxs[mi], _NT, preferred_element_type=F32)
            if masked:
                s = jnp.where(causal, s, NEG)
            m_old = m_ref[mi]
            m_new = jnp.maximum(m_old, jnp.max(s, axis=-1, keepdims=True))
            alpha = jnp.exp2(m_old - m_new)
            p = jnp.exp2(s - _rep(m_new, width // LANES))
            l_ref[mi] = alpha * l_ref[mi] + jnp.sum(p, axis=-1, keepdims=True)
            acc_ref[mi] = alpha * acc_ref[mi] + jnp.dot(
                p.astype(BF16), vb, preferred_element_type=F32)
            m_ref[mi] = m_new

    n_bulk = (qt * TQ) // TKB

    def bulk(i, carry):
        tile(pl.multiple_of(i * TKB, TKB), TKB, False)
        return carry

    def small(i, carry):
        tile(pl.multiple_of(i * TQ, TQ), TQ, False)
        return carry

    def bulk2(j, carry):
        tile(pl.multiple_of(2 * j * TKB, TKB), TKB, False)
        tile(pl.multiple_of((2 * j + 1) * TKB, TKB), TKB, False)
        return carry

    lax.fori_loop(0, n_bulk // 2, bulk2, 0)
    lax.fori_loop((n_bulk // 2) * 2, n_bulk, bulk, 0)
    lax.fori_loop(n_bulk * (TKB // TQ), qt, small, 0)
    tile(pl.multiple_of(qt * TQ, TQ), TQ, True)

    lam = _lambda(lq1_ref, lk1_ref, lq2_ref, lk2_ref, lam_init)
    o = acc_ref[0] * (1.0 / l_ref[0]) - lam * (acc_ref[1] * (1.0 / l_ref[1]))
    on = _subln(o, sub_ref[...], lam_init)
    for g in range(GS):
        z = z_ref[:, g * LANES:(g + 1) * LANES]
        o_ref[:, g * LANES:(g + 1) * LANES] = (on[g * TQ:(g + 1) * TQ] * _silu(z)).astype(BF16)


def _diff_prompt(q, k, h, lam_vecs, subln, lam_init, seq):
    gw = GS * VD_B
    n_hg = G_B // GS
    vec = pl.BlockSpec((1, HEAD_DIM), lambda c, g, t: (0, 0))
    return pl.pallas_call(
        functools.partial(_diff_prompt_body, lam_init=lam_init),
        grid=(KV_B, n_hg, seq // TQ),
        in_specs=[
            vec, vec, vec, vec,
            pl.BlockSpec((1, VD_B), lambda c, g, t: (0, 0)),
            pl.BlockSpec((TQ, gw), lambda c, g, t: (t, c * n_hg + g)),
            pl.BlockSpec((seq, VD_B), lambda c, g, t: (0, c)),
            pl.BlockSpec((seq, VD_B), lambda c, g, t: (0, COL_V // VD_B + c)),
            pl.BlockSpec((TQ, gw), lambda c, g, t: (t, COL_Z // gw + c * n_hg + g)),
        ],
        out_specs=pl.BlockSpec((TQ, gw), lambda c, g, t: (t, c * n_hg + g)),
        out_shape=jax.ShapeDtypeStruct((seq, W_Q), BF16),
        scratch_shapes=[
            pltpu.VMEM((GS * TQ, LANES), BF16),
            pltpu.VMEM((2, GS * TQ, LANES), F32),
            pltpu.VMEM((2, GS * TQ, LANES), F32),
            pltpu.VMEM((2, GS * TQ, VD_B), F32),
        ],
        compiler_params=_cparams(3),
        name="diff_prompt",
    )(*lam_vecs, subln.reshape(1, VD_B), q, k, h, h)


def _diff_sample_body(pt_ref, lq1_ref, lk1_ref, lq2_ref, lk2_ref, sub_ref,
                      qbd_ref, kn_ref, vn_ref, *rest, lam_init, dec_seq, n_chunks):
    kp_refs = rest[:NP]
    vp_refs = rest[NP:2 * NP]
    o_ref, m_ref, l_ref, acc_ref = rest[2 * NP:]
    c = pl.program_id(1)
    rows = dec_seq * G_B

    @pl.when(c == 0)
    def _():
        m_ref[...] = jnp.full(m_ref.shape, NEG, F32)
        l_ref[...] = jnp.zeros(l_ref.shape, F32)
        acc_ref[...] = jnp.zeros(acc_ref.shape, F32)

    nr = 2 * rows

    def update(ss, vbs):
        s = jnp.concatenate(ss, axis=0)
        m_old = m_ref[...]
        m_new = jnp.maximum(m_old, jnp.max(s, axis=-1, keepdims=True))
        alpha = jnp.exp2(m_old - m_new)
        pb = jnp.exp2(s - _rep(m_new, s.shape[1] // LANES)).astype(BF16)
        ones = jnp.ones((s.shape[1], LANES), BF16)
        pv = jnp.concatenate(
            [jnp.dot(pb[kv * nr:(kv + 1) * nr], jnp.concatenate([vbs[kv], ones], axis=1),
                     preferred_element_type=F32)
             for kv in range(KV_B)], axis=0)
        l_ref[...] = alpha * l_ref[...] + pv[:, VD_B:]
        acc_ref[...] = alpha * acc_ref[...] + pv[:, :VD_B]
        m_ref[...] = m_new

    ss, vbs = [], []
    for kv in range(KV_B):
        rs = pl.ds(kv, PAGE, stride=KV_B)
        kb = jnp.concatenate([r[rs, :] for r in kp_refs], axis=0).astype(BF16)
        vbs.append(jnp.concatenate([r[rs, :] for r in vp_refs], axis=0).astype(BF16))
        ss.append(lax.dot_general(qbd_ref[kv], kb, _NT, preferred_element_type=F32))
    update(ss, vbs)

    @pl.when(c == n_chunks - 1)
    def _():
        lam = _lambda(lq1_ref, lk1_ref, lq2_ref, lk2_ref, lam_init)
        r = lax.broadcasted_iota(jnp.int32, (nr, LANES), 0)
        t = (r & (rows - 1)) >> (G_B.bit_length() - 1)
        i = lax.broadcasted_iota(jnp.int32, (nr, LANES), 1)
        valid = i <= t
        zpad = jnp.zeros((LANES - 8, LANES), F32)
        ss, vbs = [], []
        for kv in range(KV_B):
            cs = slice(kv * LANES, (kv + 1) * LANES)
            kb = jnp.concatenate([kn_ref[:, cs], zpad], axis=0).astype(BF16)
            vbs.append(jnp.concatenate([vn_ref[:, cs], zpad], axis=0).astype(BF16))
            s = lax.dot_general(qbd_ref[kv], kb, _NT, preferred_element_type=F32)
            ss.append(jnp.where(valid, s, NEG))
        update(ss, vbs)
        on = acc_ref[...] * (1.0 / l_ref[...])
        for kv in range(KV_B):
            o = on[kv * nr:kv * nr + rows] - lam * on[kv * nr + rows:(kv + 1) * nr]
            o_ref[kv] = _subln(o, sub_ref[...], lam_init)


def _diff_sample(pt_flat, qbd, kn8, vn8, cache_k, cache_v, lam_vecs, subln, lam_init,
                 dec_batch, dec_seq, n_pages):
    rows = dec_seq * G_B
    n_chunks = n_pages // NP
    wkv = KV_B * VD_B
    vec = pl.BlockSpec((1, HEAD_DIM), lambda b, c, pt: (0, 0))

    def page_spec(i):
        return pl.BlockSpec((None, PAGE * KV_B, VD_B),
                            lambda b, c, pt: (pt[b * n_pages + c * NP + i], 0, 0))

    grid_spec = pltpu.PrefetchScalarGridSpec(
        num_scalar_prefetch=1,
        grid=(dec_batch, n_chunks),
        in_specs=[
            vec, vec, vec, vec,
            pl.BlockSpec((1, VD_B), lambda b, c, pt: (0, 0)),
            pl.BlockSpec((None, KV_B, 2 * rows, LANES), lambda b, c, pt: (b, 0, 0, 0)),
            pl.BlockSpec((None, 8, wkv), lambda b, c, pt: (b, 0, 0)),
            pl.BlockSpec((None, 8, wkv), lambda b, c, pt: (b, 0, 0)),
        ] + [page_spec(i) for i in range(NP)] + [page_spec(i) for i in range(NP)],
        out_specs=pl.BlockSpec((None, KV_B, rows, VD_B), lambda b, c, pt: (b, 0, 0, 0)),
        scratch_shapes=[
            pltpu.VMEM((KV_B * 2 * rows, LANES), F32),
            pltpu.VMEM((KV_B * 2 * rows, LANES), F32),
            pltpu.VMEM((KV_B * 2 * rows, VD_B), F32),
        ],
    )
    return pl.pallas_call(
        functools.partial(_diff_sample_body, lam_init=lam_init, dec_seq=dec_seq,
                          n_chunks=n_chunks),
        grid_spec=grid_spec,
        out_shape=jax.ShapeDtypeStruct((dec_batch, KV_B, rows, VD_B), F32),
        compiler_params=_cparams(2),
        name="diff_sample",
    )(pt_flat, *lam_vecs, subln.reshape(1, VD_B), qbd, kn8, vn8,
      *([cache_k] * NP), *([cache_v] * NP))


def _gate_body(o_ref, z_ref, g_ref):
    g_ref[...] = (o_ref[...] * _silu(z_ref[...])).astype(BF16)


def _gate_sample(o_s, h, row_blk):
    n = o_s.shape[0]
    return pl.pallas_call(
        _gate_body,
        grid=(W_Z // 1024,),
        in_specs=[
            pl.BlockSpec((n, 1024), lambda c: (0, c)),
            pl.BlockSpec((n, 1024), lambda c: (row_blk, COL_Z // 1024 + c)),
        ],
        out_specs=pl.BlockSpec((n, 1024), lambda c: (0, c)),
        out_shape=jax.ShapeDtypeStruct((n, W_Z), BF16),
        compiler_params=_cparams(1),
        name="gate_sample",
    )(o_s, h)


def _outproj_body(gp_ref, gs_ref, w_ref, xp_ref, xs_ref, yp_ref, ys_ref, wb_ref,
                  *, n_prompt_tiles):
    i = pl.program_id(1)

    @pl.when(i == 0)
    def _():
        wb_ref[...] = w_ref[...].astype(BF16)

    @pl.when(i < n_prompt_tiles)
    def _():
        yp_ref[...] = xp_ref[...] + jnp.dot(gp_ref[...], wb_ref[...], preferred_element_type=F32)

    @pl.when(i >= n_prompt_tiles)
    def _():
        ys_ref[...] = xs_ref[...] + jnp.dot(gs_ref[...], wb_ref[...], preferred_element_type=F32)


def _outproj(gp, gs, w, xp, xs):
    seq, n_dec = gp.shape[0], gs.shape[0]
    assert n_dec == TM and seq % TM == 0
    n_prompt_tiles = seq // TM
    last = n_prompt_tiles - 1
    return pl.pallas_call(
        functools.partial(_outproj_body, n_prompt_tiles=n_prompt_tiles),
        grid=(D_MODEL // TN_OUT, n_prompt_tiles + 1),
        in_specs=[
            pl.BlockSpec((TM, W_Z), lambda j, i: (jnp.minimum(i, last), 0)),
            pl.BlockSpec((TM, W_Z), lambda j, i: (0, 0)),
            pl.BlockSpec((W_Z, TN_OUT), lambda j, i: (0, j)),
            pl.BlockSpec((TM, TN_OUT), lambda j, i: (jnp.minimum(i, last), j)),
            pl.BlockSpec((TM, TN_OUT), lambda j, i: (0, j)),
        ],
        out_specs=[
            pl.BlockSpec((TM, TN_OUT), lambda j, i: (jnp.minimum(i, last), j)),
            pl.BlockSpec((TM, TN_OUT), lambda j, i: (0, j)),
        ],
        out_shape=[
            jax.ShapeDtypeStruct((seq, D_MODEL), F32),
            jax.ShapeDtypeStruct((n_dec, D_MODEL), F32),
        ],
        scratch_shapes=[pltpu.VMEM((W_Z, TN_OUT), BF16)],
        compiler_params=_cparams(2),
        name="outproj",
    )(gp, gs, w, xp, xs)


def _rope_tables(seq, dec_batch, dec_seq, past_len):
    half = HEAD_DIM // 2
    inv_freq = 1.0 / (ROPE_THETA ** (jnp.arange(0, HEAD_DIM, 2, dtype=F32) / HEAD_DIM))
    pos = jnp.concatenate([
        jnp.arange(seq, dtype=jnp.int32),
        jnp.tile(past_len + jnp.arange(dec_seq, dtype=jnp.int32), dec_batch),
    ])
    ang = pos.astype(F32)[:, None] * inv_freq[None, :]
    cos = jnp.cos(ang)
    sin = jnp.sin(ang)
    reps = LANES // HEAD_DIM
    cos_t = jnp.concatenate([cos, cos] * reps, axis=1)
    sin_t = jnp.concatenate([-sin, sin] * reps, axis=1)
    assert cos_t.shape[1] == LANES and half * 2 == HEAD_DIM
    return cos_t, sin_t


def kernel(x_prompt, x_sample, cache_win_k, cache_win_v, cache_k, cache_v, page_table,
           norm_a, w_in_a, q_norm_a, k_norm_a, sinks_a, w_out_a,
           norm_b, w_in_b, q_norm_b, k_norm_b, lambda_q1, lambda_k1, lambda_q2, lambda_k2,
           subln_b, w_out_b):
    batch, seq, _ = x_prompt.shape
    dec_batch, dec_seq, _ = x_sample.shape
    n_pool = cache_k.shape[1]
    n_pages = page_table.shape[1]
    past_len = n_pages * cache_k.shape[2]
    n_dec = dec_batch * dec_seq
    n_rows = seq + n_dec
    assert batch == 1 and n_dec == TM and seq % TM == 0 and cache_k.shape[2] == PAGE
    assert cache_win_k.shape[2] == WINDOW and n_pages % NP == 0
    s_blk = seq // TM

    cos_t, sin_t = _rope_tables(seq, dec_batch, dec_seq, past_len)
    xp = x_prompt.reshape(seq, D_MODEL)
    xs = x_sample.reshape(n_dec, D_MODEL)

    h = _inproj(xp, xs, norm_a[0], w_in_a[0].astype(BF16))
    q, k = _qkpost(h, cos_t, sin_t, q_norm_a[0], k_norm_a[0], seq, n_rows, paged=False)
    v = h[:, COL_V:COL_V + W_V]

    gp = _win_prompt(q, k, h, sinks_a[0], seq)

    rows_a = dec_seq * G_A
    qs = q[seq:].reshape(dec_batch, dec_seq, KV_A, G_A, HEAD_DIM)
    qs = qs.transpose(0, 2, 1, 3, 4).reshape(dec_batch, KV_A, rows_a, HEAD_DIM)
    sink_tab = jnp.broadcast_to(sinks_a[0].astype(F32).reshape(KV_A, 1, G_A, 1),
                                (KV_A, dec_seq, G_A, LANES)).reshape(KV_A * rows_a, LANES)
    wkv_a = KV_A * HEAD_DIM
    kn = k[seq:].reshape(dec_batch, dec_seq, wkv_a)
    vn = v[seq:].reshape(dec_batch, dec_seq, wkv_a)
    o_s, win_k_s, win_v_s = _win_sample(
        qs, sink_tab,
        cache_win_k[0].transpose(0, 2, 3, 1).reshape(dec_batch, wkv_a, WINDOW),
        cache_win_v[0].transpose(0, 2, 3, 1).reshape(dec_batch, wkv_a, WINDOW),
        kn, vn, dec_batch, dec_seq)
    o_s = o_s.reshape(dec_batch, KV_A, dec_seq, G_A * HEAD_DIM)
    o_s = o_s.transpose(0, 2, 1, 3).reshape(n_dec, W_Q)
    gs = _gate_sample(o_s, h, s_blk)

    x1p, x1s = _outproj(gp, gs, w_out_a[0], xp, xs)

    wl = min(WINDOW, seq)
    win_k_p = k[seq - wl:seq].reshape(1, batch, wl, KV_A, HEAD_DIM)
    win_v_p = v[seq - wl:seq].reshape(1, batch, wl, KV_A, HEAD_DIM)
    win_k_s = win_k_s.reshape(dec_batch, KV_A, HEAD_DIM, WINDOW).transpose(0, 3, 1, 2)[None]
    win_v_s = win_v_s.reshape(dec_batch, KV_A, HEAD_DIM, WINDOW).transpose(0, 3, 1, 2)[None]

    lam_init = 0.8 - 0.6 * math.exp(-0.3 * 1)
    lam_vecs = [a[0].astype(F32).reshape(1, HEAD_DIM)
                for a in (lambda_q1, lambda_k1, lambda_q2, lambda_k2)]

    h = _inproj(x1p, x1s, norm_b[0], w_in_b[0].astype(BF16))
    q, k, kp_pg, vp_pg, ks_pg, vs_pg = _qkpost(h, cos_t, sin_t, q_norm_b[0], k_norm_b[0],
                                               seq, n_rows, paged=True)
    v = h[:, COL_V:COL_V + W_V]

    gp = _diff_prompt(q, k, h, lam_vecs, subln_b[0].astype(F32), lam_init, seq)

    rows_b = dec_seq * G_B
    qs = q[seq:].reshape(dec_batch, dec_seq, KV_B, G_B, 2, HEAD_DIM)
    qs = qs.transpose(0, 2, 4, 1, 3, 5).reshape(dec_batch, KV_B, 2, rows_b, HEAD_DIM)
    zq = jnp.zeros_like(qs[:, :, 0])
    qbd = jnp.concatenate([
        jnp.concatenate([qs[:, :, 0], zq], axis=-1),
        jnp.concatenate([zq, qs[:, :, 1]], axis=-1),
    ], axis=2)
    wkv_b = KV_B * VD_B
    kn = k[seq:].reshape(dec_batch, dec_seq, wkv_b)
    vn = v[seq:].reshape(dec_batch, dec_seq, wkv_b)
    padn = jnp.zeros((dec_batch, 8 - dec_seq, wkv_b), F32)
    kn8 = jnp.concatenate([kn, padn], axis=1)
    vn8 = jnp.concatenate([vn, padn], axis=1)
    o_s = _diff_sample(
        page_table.reshape(-1).astype(jnp.int32), qbd, kn8, vn8,
        cache_k[0].reshape(n_pool, PAGE * KV_B, VD_B), cache_v[0].reshape(n_pool, PAGE * KV_B, VD_B),
        lam_vecs, subln_b[0].astype(F32), lam_init, dec_batch, dec_seq, n_pages)
    o_s = o_s.reshape(dec_batch, KV_B, dec_seq, G_B * VD_B)
    o_s = o_s.transpose(0, 2, 1, 3).reshape(n_dec, W_Q)
    gs = _gate_sample(o_s, h, s_blk)

    x2p, x2s = _outproj(gp, gs, w_out_b[0], x1p, x1s)

    y_prompt = x2p.reshape(batch, seq, D_MODEL)
    y_sample = x2s.reshape(dec_batch, dec_seq, D_MODEL)
    kv_k_p = kp_pg.reshape(1, batch, seq // PAGE, PAGE, KV_B, 2 * HEAD_DIM)
    kv_v_p = vp_pg.reshape(1, batch, seq // PAGE, PAGE, KV_B, VD_B)
    kv_k_s = ks_pg.reshape(1, dec_batch, dec_seq, KV_B, 2 * HEAD_DIM)
    kv_v_s = vs_pg.reshape(1, dec_batch, dec_seq, KV_B, VD_B)

    return (y_prompt, y_sample, win_k_p, win_v_p, win_k_s, win_v_s,
            kv_k_p, kv_v_p, kv_k_s, kv_v_s)
```

```python
import functools
import math

import jax
import jax.numpy as jnp
from jax import lax
from jax.experimental import pallas as pl
from jax.experimental.pallas import tpu as pltpu

F32 = jnp.float32
BF16 = jnp.bfloat16

D_MODEL = 2048
HEAD_DIM = 64
ROPE_THETA = 10000.0
EPS = 1e-6
WINDOW = 128
KV_A = 8
G_A = 8
KV_B = 4
G_B = 8
VD_B = 128
PAGE = 128
NEG = -1e30

W_Q = 4096
W_K = 512
W_V = 512
W_Z = 4096
W_IN = W_Q + W_K + W_V + W_Z
COL_K = W_Q
COL_V = W_Q + W_K
COL_Z = W_Q + W_K + W_V

LANES = 128
VMEM_LIMIT = 56 * 1024 * 1024

TM = 512
TN_IN = 2304
TN_OUT = 512
TQ = 256
TKB = 1024
GS = 8
NP = 32

LOG2E = math.log2(math.e)
Q_SCALE = HEAD_DIM ** -0.5 * LOG2E

_NT = (((1,), (1,)), ((), ()))


def _cparams(n_axes, flags=None):
    return pltpu.CompilerParams(
        dimension_semantics=("arbitrary",) * n_axes,
        vmem_limit_bytes=VMEM_LIMIT,
        flags=flags,
    )


def _silu(z):
    return z * (1.0 / (1.0 + jnp.exp(-z)))


def _inproj_body(xp_ref, xs_ref, g_ref, w_ref, h_ref, xn_ref, *, n_prompt_tiles):
    i = pl.program_id(0)
    j = pl.program_id(1)

    def norm(x_ref):
        x = x_ref[...]
        ms = jnp.mean(x * x, axis=-1, keepdims=True)
        xn_ref[...] = (x * lax.rsqrt(ms + EPS) * g_ref[...]).astype(BF16)

    @pl.when((j == 0) & (i < n_prompt_tiles))
    def _():
        norm(xp_ref)

    @pl.when((j == 0) & (i >= n_prompt_tiles))
    def _():
        norm(xs_ref)

    h_ref[...] = jnp.dot(xn_ref[...], w_ref[...], preferred_element_type=F32)


def _inproj(xp, xs, gain, w_bf16):
    seq, n_dec = xp.shape[0], xs.shape[0]
    assert n_dec == TM and seq % TM == 0
    n_prompt_tiles = seq // TM
    n_rows = seq + n_dec
    grid = (n_rows // TM, W_IN // TN_IN)
    return pl.pallas_call(
        functools.partial(_inproj_body, n_prompt_tiles=n_prompt_tiles),
        grid=grid,
        in_specs=[
            pl.BlockSpec((TM, D_MODEL), lambda i, j: (jnp.minimum(i, n_prompt_tiles - 1), 0)),
            pl.BlockSpec((TM, D_MODEL), lambda i, j: (0, 0)),
            pl.BlockSpec((1, D_MODEL), lambda i, j: (0, 0)),
            pl.BlockSpec((D_MODEL, TN_IN), lambda i, j: (0, j)),
        ],
        out_specs=pl.BlockSpec((TM, TN_IN), lambda i, j: (i, j)),
        out_shape=jax.ShapeDtypeStruct((n_rows, W_IN), F32),
        scratch_shapes=[pltpu.VMEM((TM, D_MODEL), BF16)],
        compiler_params=_cparams(2),
        name="inproj",
    )(xp, xs, gain.reshape(1, D_MODEL), w_bf16)


TR = 256
CW = 256


def _qkpost_body(h_ref, cos_ref, sin_ref, qg_ref, kg_ref, bd_ref, q_ref, k_ref, *paged_refs,
                 n_prompt_tiles):
    cos = jnp.concatenate([cos_ref[...]] * (CW // LANES), axis=1)
    sin = jnp.concatenate([sin_ref[...]] * (CW // LANES), axis=1)
    bd = bd_ref[...]
    lane = lax.broadcasted_iota(jnp.int32, (TR, CW), 1)
    first_half = (lane & (HEAD_DIM - 1)) < (HEAD_DIM // 2)

    def normrope(c, gain):
        t = h_ref[:, c * CW:(c + 1) * CW]
        ss = jnp.dot((t * t).astype(BF16), bd, preferred_element_type=F32)
        tn = t * lax.rsqrt(ss * (1.0 / HEAD_DIM) + EPS) * gain
        partner = jnp.where(first_half,
                            pltpu.roll(tn, CW - HEAD_DIM // 2, 1),
                            pltpu.roll(tn, HEAD_DIM // 2, 1))
        return tn * cos + partner * sin

    qg = qg_ref[...]
    kg = kg_ref[...]
    for c in range(W_Q // CW):
        q_ref[:, c * CW:(c + 1) * CW] = (normrope(c, qg) * Q_SCALE).astype(BF16)
    for c in range(W_K // CW):
        k_ref[:, c * CW:(c + 1) * CW] = normrope(W_Q // CW + c, kg)
    if paged_refs:
        def write_paged(k_dst, v_dst):
            for u in range(KV_B):
                rs = pl.ds(u, TR, stride=KV_B)
                k_dst[rs, :] = k_ref[:, u * VD_B:(u + 1) * VD_B]
                v_dst[rs, :] = h_ref[:, COL_V + u * VD_B:COL_V + (u + 1) * VD_B]

        is_prompt = pl.program_id(0) < n_prompt_tiles
        pl.when(is_prompt)(functools.partial(write_paged, paged_refs[0], paged_refs[1]))
        pl.when(jnp.logical_not(is_prompt))(
            functools.partial(write_paged, paged_refs[2], paged_refs[3]))


def _qkpost(h, cos, sin, q_gain, k_gain, seq, n_rows, paged):
    head = jnp.arange(CW, dtype=jnp.int32) // HEAD_DIM
    bd = (head[:, None] == head[None, :]).astype(BF16)
    qg = jnp.tile(q_gain.astype(F32), CW // HEAD_DIM).reshape(1, CW)
    kg = jnp.tile(k_gain.astype(F32), CW // HEAD_DIM).reshape(1, CW)
    in_w = W_Q + W_K + (W_V if paged else 0)
    assert seq % TR == 0 and (n_rows - seq) % TR == 0
    n_pt = seq // TR
    paged_specs, paged_shapes = [], []
    if paged:
        paged_specs = (
            [pl.BlockSpec((TR * KV_B, VD_B), lambda i: (jnp.minimum(i, n_pt - 1), 0))] * 2
            + [pl.BlockSpec((TR * KV_B, VD_B), lambda i: (jnp.maximum(i - n_pt, 0), 0))] * 2)
        paged_shapes = ([jax.ShapeDtypeStruct((seq * KV_B, VD_B), F32)] * 2
                        + [jax.ShapeDtypeStruct(((n_rows - seq) * KV_B, VD_B), F32)] * 2)
    return pl.pallas_call(
        functools.partial(_qkpost_body, n_prompt_tiles=n_pt),
        grid=(n_rows // TR,),
        in_specs=[
            pl.BlockSpec((TR, in_w), lambda i: (i, 0)),
            pl.BlockSpec((TR, LANES), lambda i: (i, 0)),
            pl.BlockSpec((TR, LANES), lambda i: (i, 0)),
            pl.BlockSpec((1, CW), lambda i: (0, 0)),
            pl.BlockSpec((1, CW), lambda i: (0, 0)),
            pl.BlockSpec((CW, CW), lambda i: (0, 0)),
        ],
        out_specs=[
            pl.BlockSpec((TR, W_Q), lambda i: (i, 0)),
            pl.BlockSpec((TR, W_K), lambda i: (i, 0)),
        ] + paged_specs,
        out_shape=[
            jax.ShapeDtypeStruct((n_rows, W_Q), BF16),
            jax.ShapeDtypeStruct((n_rows, W_K), F32),
        ] + paged_shapes,
        compiler_params=_cparams(1),
        name="qkpost",
    )(h, cos, sin, qg, kg, bd)


def _half_split(x, lo_half, even):
    if even:
        a = jnp.where(lo_half, x, 0.0)
        return a, pltpu.roll(a, HEAD_DIM, 1)
    b = jnp.where(lo_half, 0.0, x)
    return pltpu.roll(b, HEAD_DIM, 1), b


def _win_prompt_body(sink_ref, q_ref, kp_ref, kc_ref, vp_ref, vc_ref,
                     z0_ref, z1_ref, z2_ref, z3_ref, o_ref):
    n = pl.program_id(0)
    z_refs = (z0_ref, z1_ref, z2_ref, z3_ref)
    nk = 2 * WINDOW
    lo_half = lax.broadcasted_iota(jnp.int32, (nk, LANES), 1) < HEAD_DIM
    rows = 4 * WINDOW
    qi = lax.broadcasted_iota(jnp.int32, (rows, nk), 0) & (WINDOW - 1)
    ki = lax.broadcasted_iota(jnp.int32, (rows, nk), 1)
    valid = (ki > qi) & (ki <= qi + WINDOW) & (ki >= jnp.where(n > 0, 0, WINDOW))

    ones_k = jnp.ones((nk, LANES), BF16)

    def softmax(s, sink):
        s = jnp.where(valid, s, NEG)
        m = jnp.maximum(jnp.broadcast_to(jnp.max(s, axis=-1, keepdims=True), sink.shape), sink)
        p = jnp.exp2(s - _rep(m, nk // LANES)).astype(BF16)
        l = jnp.dot(p, ones_k, preferred_element_type=F32) + jnp.exp2(sink - m)
        return p, 1.0 / l

    for pr in range(KV_A // 2):
        cs = slice(pr * LANES, (pr + 1) * LANES)
        kk = jnp.concatenate([kp_ref[:, cs], kc_ref[:, cs]], axis=0)
        vv = jnp.concatenate([vp_ref[:, cs], vc_ref[:, cs]], axis=0)
        for sub in range(2):
            kv = 2 * pr + sub
            ka, kb = _half_split(kk, lo_half, sub == 0)
            va, vb = _half_split(vv, lo_half, sub == 0)
            base = kv * G_A * HEAD_DIM
            q4 = jnp.concatenate(
                [q_ref[:, base + j * LANES: base + (j + 1) * LANES] for j in range(4)], axis=0)
            s_e = lax.dot_general(q4, ka.astype(BF16), _NT, preferred_element_type=F32)
            s_o = lax.dot_general(q4, kb.astype(BF16), _NT, preferred_element_type=F32)
            sink_e = jnp.concatenate(
                [jnp.full((WINDOW, LANES), sink_ref[kv * G_A + 2 * j] * LOG2E, F32)
                 for j in range(4)], axis=0)
            sink_o = jnp.concatenate(
                [jnp.full((WINDOW, LANES), sink_ref[kv * G_A + 2 * j + 1] * LOG2E, F32)
                 for j in range(4)], axis=0)
            p_e, r_e = softmax(s_e, sink_e)
            p_o, r_o = softmax(s_o, sink_o)
            o = (jnp.dot(p_e, va.astype(BF16), preferred_element_type=F32) * r_e
                 + jnp.dot(p_o, vb.astype(BF16), preferred_element_type=F32) * r_o)
            zr = z_refs[kv // 2]
            zb = (kv % 2) * 512
            for j in range(4):
                z = zr[:, zb + j * LANES: zb + (j + 1) * LANES]
                o_ref[:, base + j * LANES: base + (j + 1) * LANES] = (
                    o[j * WINDOW:(j + 1) * WINDOW] * _silu(z)).astype(BF16)


def _win_prompt(q, k, h, sinks, seq):
    nb = seq // WINDOW
    prev = lambda n: (jnp.maximum(n - 1, 0), 0)
    zspec = lambda c: pl.BlockSpec((WINDOW, 1024), lambda n: (n, COL_Z // 1024 + c))
    return pl.pallas_call(
        _win_prompt_body,
        grid=(nb,),
        in_specs=[
            pl.BlockSpec(memory_space=pltpu.SMEM),
            pl.BlockSpec((WINDOW, W_Q), lambda n: (n, 0)),
            pl.BlockSpec((WINDOW, W_K), prev),
            pl.BlockSpec((WINDOW, W_K), lambda n: (n, 0)),
            pl.BlockSpec((WINDOW, W_V), lambda n: (jnp.maximum(n - 1, 0), COL_V // W_V)),
            pl.BlockSpec((WINDOW, W_V), lambda n: (n, COL_V // W_V)),
            zspec(0), zspec(1), zspec(2), zspec(3),
        ],
        out_specs=pl.BlockSpec((WINDOW, W_Q), lambda n: (n, 0)),
        out_shape=jax.ShapeDtypeStruct((seq, W_Q), BF16),
        compiler_params=_cparams(1),
        name="win_prompt",
    )(sinks.astype(F32), q, k, k, h, h, h, h, h, h)


def _win_sample_body(q_ref, sink_ref, ck_ref, cv_ref, kn_ref, vn_ref,
                     o_ref, ok_ref, ov_ref, kpad_ref, vpad_ref, *, dec_seq):
    keep = WINDOW - dec_seq
    wkv = KV_A * HEAD_DIM
    lane = lax.broadcasted_iota(jnp.int32, (wkv, WINDOW), 1)

    for pad, new in ((kpad_ref, kn_ref), (vpad_ref, vn_ref)):
        pad[...] = jnp.zeros(pad.shape, F32)
        pad[0:dec_seq, :] = new[...]
    knt = kpad_ref[...].T
    vnt = vpad_ref[...].T

    ok_ref[...] = jnp.where(lane >= keep, pltpu.roll(knt, keep, 1), pltpu.roll(ck_ref[...], keep, 1))
    ov_ref[...] = jnp.where(lane >= keep, pltpu.roll(vnt, keep, 1), pltpu.roll(cv_ref[...], keep, 1))

    rows = dec_seq * G_A
    all_rows = KV_A * rows
    t = (lax.broadcasted_iota(jnp.int32, (all_rows, WINDOW), 0) & (rows - 1)) >> (
        G_A.bit_length() - 1)
    i = lax.broadcasted_iota(jnp.int32, (all_rows, WINDOW), 1)
    valid_old = i > t
    valid_new = i <= t

    s_old, s_new = [], []
    for kv in range(KV_A):
        hs = slice(kv * HEAD_DIM, (kv + 1) * HEAD_DIM)
        q = q_ref[kv]
        s_old.append(jnp.dot(q, ck_ref[hs, :].astype(BF16), preferred_element_type=F32))
        s_new.append(jnp.dot(q, knt[hs, :].astype(BF16), preferred_element_type=F32))
    s = jnp.concatenate([jnp.where(valid_old, jnp.concatenate(s_old, axis=0), NEG),
                         jnp.where(valid_new, jnp.concatenate(s_new, axis=0), NEG)], axis=1)
    sink = sink_ref[...] * LOG2E
    m = jnp.maximum(jnp.broadcast_to(jnp.max(s, axis=-1, keepdims=True), sink.shape), sink)
    p = jnp.exp2(s - _rep(m, 2)).astype(BF16)
    l = jnp.dot(p, jnp.ones((2 * WINDOW, LANES), BF16), preferred_element_type=F32) + jnp.exp2(
        sink - m)
    r = 1.0 / l
    for kv in range(KV_A):
        hs = slice(kv * HEAD_DIM, (kv + 1) * HEAD_DIM)
        rs = slice(kv * rows, (kv + 1) * rows)
        pv = (lax.dot_general(p[rs, :WINDOW], cv_ref[hs, :].astype(BF16), _NT,
                              preferred_element_type=F32)
              + lax.dot_general(p[rs, WINDOW:], vnt[hs, :].astype(BF16), _NT,
                                preferred_element_type=F32))
        o_ref[kv] = pv * r[rs, :HEAD_DIM]


def _win_sample(q, sink_tab, ck, cv, kn, vn, dec_batch, dec_seq):
    rows = dec_seq * G_A
    wkv = KV_A * HEAD_DIM
    bspec = lambda shp: pl.BlockSpec((None,) + shp, lambda b: (b,) + (0,) * len(shp))
    return pl.pallas_call(
        functools.partial(_win_sample_body, dec_seq=dec_seq),
        grid=(dec_batch,),
        in_specs=[
            bspec((KV_A, rows, HEAD_DIM)),
            pl.BlockSpec((KV_A * rows, LANES), lambda b: (0, 0)),
            bspec((wkv, WINDOW)), bspec((wkv, WINDOW)),
            bspec((dec_seq, wkv)), bspec((dec_seq, wkv)),
        ],
        out_specs=[
            bspec((KV_A, rows, HEAD_DIM)),
            bspec((wkv, WINDOW)), bspec((wkv, WINDOW)),
        ],
        out_shape=[
            jax.ShapeDtypeStruct((dec_batch, KV_A, rows, HEAD_DIM), F32),
            jax.ShapeDtypeStruct((dec_batch, wkv, WINDOW), F32),
            jax.ShapeDtypeStruct((dec_batch, wkv, WINDOW), F32),
        ],
        scratch_shapes=[pltpu.VMEM((LANES, wkv), F32), pltpu.VMEM((LANES, wkv), F32)],
        compiler_params=_cparams(1),
        name="win_sample",
    )(q, sink_tab, ck, cv, kn, vn)


def _lambda(lq1_ref, lk1_ref, lq2_ref, lk2_ref, lam_init):
    a = jnp.sum(lq1_ref[...] * lk1_ref[...], axis=-1, keepdims=True)
    b = jnp.sum(lq2_ref[...] * lk2_ref[...], axis=-1, keepdims=True)
    return jnp.exp(a) - jnp.exp(b) + lam_init


def _subln(o, gain, lam_init):
    ms = jnp.mean(o * o, axis=-1, keepdims=True)
    return o * lax.rsqrt(ms + EPS) * gain * (1.0 - lam_init)


def _rep(x, n):
    return jnp.concatenate([x] * n, axis=1) if n > 1 else x


def _diff_prompt_body(lq1_ref, lk1_ref, lq2_ref, lk2_ref, sub_ref,
                      q_ref, k_ref, v_ref, z_ref, o_ref,
                      q8_ref, m_ref, l_ref, acc_ref, *, lam_init):
    qt = pl.program_id(2)
    rows = GS * TQ
    for g in range(GS):
        q8_ref[g * TQ:(g + 1) * TQ, :] = q_ref[:, g * LANES:(g + 1) * LANES]

    def tile(koff, width, masked, first=False):
        kf = k_ref[pl.ds(koff, width), :]
        vb = v_ref[pl.ds(koff, width), :].astype(BF16)
        lo_half = lax.broadcasted_iota(jnp.int32, (width, LANES), 1) < HEAD_DIM
        kxs = (jnp.where(lo_half, kf, 0.0).astype(BF16),
               jnp.where(lo_half, 0.0, kf).astype(BF16))
        q8 = q8_ref[...]
        if masked:
            qi = lax.broadcasted_iota(jnp.int32, (rows, width), 0) & (TQ - 1)
            ki = lax.broadcasted_iota(jnp.int32, (rows, width), 1)
            causal = ki <= qi
        for mi in range(2):
            s = lax.dot_general(q8, kxs[mi], _NT, preferred_element_type=F32)
            if masked:
                s = jnp.where(causal, s, NEG)
            if first:
                m_new = jnp.broadcast_to(jnp.max(s, axis=-1, keepdims=True), (rows, LANES))
                p = jnp.exp2(s - _rep(m_new, width // LANES))
                l_ref[mi] = jnp.broadcast_to(jnp.sum(p, axis=-1, keepdims=True), (rows, LANES))
                acc_ref[mi] = jnp.dot(p.astype(BF16), vb, preferred_element_type=F32)
                m_ref[mi] = m_new
                continue
            m_old = m_ref[mi]
            m_new = jnp.maximum(m_old, jnp.max(s, axis=-1, keepdims=True))
            alpha = jnp.exp2(m_old - m_new)
            p = jnp.exp2(s - _rep(m_new, width // LANES))
            l_ref[mi] = alpha * l_ref[mi] + jnp.sum(p, axis=-1, keepdims=True)
            acc_ref[mi] = alpha * acc_ref[mi] + jnp.dot(
                p.astype(BF16), vb, preferred_element_type=F32)
            m_ref[mi] = m_new

    tile(pl.multiple_of(qt * TQ, TQ), TQ, True, first=True)

    n_bulk = (qt * TQ) // TKB

    def bulk(i, carry):
        tile(pl.multiple_of(i * TKB, TKB), TKB, False)
        return carry

    def small(i, carry):
        tile(pl.multiple_of(i * TQ, TQ), TQ, False)
        return carry

    def bulk2(j, carry):
        tile(pl.multiple_of(2 * j * TKB, TKB), TKB, False)
        tile(pl.multiple_of((2 * j + 1) * TKB, TKB), TKB, False)
        return carry

    lax.fori_loop(0, n_bulk // 2, bulk2, 0)
    lax.fori_loop((n_bulk // 2) * 2, n_bulk, bulk, 0)
    lax.fori_loop(n_bulk * (TKB // TQ), qt, small, 0)

    lam = _lambda(lq1_ref, lk1_ref, lq2_ref, lk2_ref, lam_init)
    o = acc_ref[0] * (1.0 / l_ref[0]) - lam * (acc_ref[1] * (1.0 / l_ref[1]))
    on = _subln(o, sub_ref[...], lam_init)
    for g in range(GS):
        z = z_ref[:, g * LANES:(g + 1) * LANES]
        o_ref[:, g * LANES:(g + 1) * LANES] = (on[g * TQ:(g + 1) * TQ] * _silu(z)).astype(BF16)


def _diff_prompt(q, k, h, lam_vecs, subln, lam_init, seq):
    gw = GS * VD_B
    n_hg = G_B // GS
    vec = pl.BlockSpec((1, HEAD_DIM), lambda c, g, t: (0, 0))
    return pl.pallas_call(
        functools.partial(_diff_prompt_body, lam_init=lam_init),
        grid=(KV_B, n_hg, seq // TQ),
        in_specs=[
            vec, vec, vec, vec,
            pl.BlockSpec((1, VD_B), lambda c, g, t: (0, 0)),
            pl.BlockSpec((TQ, gw), lambda c, g, t: (t, c * n_hg + g)),
            pl.BlockSpec((seq, VD_B), lambda c, g, t: (0, c)),
            pl.BlockSpec((seq, VD_B), lambda c, g, t: (0, COL_V // VD_B + c)),
            pl.BlockSpec((TQ, gw), lambda c, g, t: (t, COL_Z // gw + c * n_hg + g)),
        ],
        out_specs=pl.BlockSpec((TQ, gw), lambda c, g, t: (t, c * n_hg + g)),
        out_shape=jax.ShapeDtypeStruct((seq, W_Q), BF16),
        scratch_shapes=[
            pltpu.VMEM((GS * TQ, LANES), BF16),
            pltpu.VMEM((2, GS * TQ, LANES), F32),
            pltpu.VMEM((2, GS * TQ, LANES), F32),
            pltpu.VMEM((2, GS * TQ, VD_B), F32),
        ],
        compiler_params=_cparams(3),
        name="diff_prompt",
    )(*lam_vecs, subln.reshape(1, VD_B), q, k, h, h)


def _diff_sample_body(pt_ref, lq1_ref, lk1_ref, lq2_ref, lk2_ref, sub_ref,
                      qbd_ref, kn_ref, vn_ref, *rest, lam_init, dec_seq, n_chunks):
    kp_refs = rest[:NP]
    vp_refs = rest[NP:2 * NP]
    o_ref, m_ref, l_ref, acc_ref = rest[2 * NP:]
    c = pl.program_id(1)
    rows = dec_seq * G_B

    @pl.when(c == 0)
    def _():
        m_ref[...] = jnp.full(m_ref.shape, NEG, F32)
        l_ref[...] = jnp.zeros(l_ref.shape, F32)
        acc_ref[...] = jnp.zeros(acc_ref.shape, F32)

    nr = 2 * rows

    def update(ss, vbs):
        s = jnp.concatenate(ss, axis=0)
        m_old = m_ref[...]
        m_new = jnp.maximum(m_old, jnp.max(s, axis=-1, keepdims=True))
        alpha = jnp.exp2(m_old - m_new)
        pb = jnp.exp2(s - _rep(m_new, s.shape[1] // LANES)).astype(BF16)
        ones = jnp.ones((s.shape[1], LANES), BF16)
        pv = jnp.concatenate(
            [jnp.dot(pb[kv * nr:(kv + 1) * nr], jnp.concatenate([vbs[kv], ones], axis=1),
                     preferred_element_type=F32)
             for kv in range(KV_B)], axis=0)
        l_ref[...] = alpha * l_ref[...] + pv[:, VD_B:]
        acc_ref[...] = alpha * acc_ref[...] + pv[:, :VD_B]
        m_ref[...] = m_new

    ss, vbs = [], []
    for kv in range(KV_B):
        rs = pl.ds(kv, PAGE, stride=KV_B)
        kb = jnp.concatenate([r[rs, :] for r in kp_refs], axis=0).astype(BF16)
        vbs.append(jnp.concatenate([r[rs, :] for r in vp_refs], axis=0).astype(BF16))
        ss.append(lax.dot_general(qbd_ref[kv], kb, _NT, preferred_element_type=F32))
    update(ss, vbs)

    @pl.when(c == n_chunks - 1)
    def _():
        lam = _lambda(lq1_ref, lk1_ref, lq2_ref, lk2_ref, lam_init)
        r = lax.broadcasted_iota(jnp.int32, (nr, LANES), 0)
        t = (r & (rows - 1)) >> (G_B.bit_length() - 1)
        i = lax.broadcasted_iota(jnp.int32, (nr, LANES), 1)
        valid = i <= t
        zpad = jnp.zeros((LANES - 8, LANES), F32)
        ss, vbs = [], []
        for kv in range(KV_B):
            cs = slice(kv * LANES, (kv + 1) * LANES)
            kb = jnp.concatenate([kn_ref[:, cs], zpad], axis=0).astype(BF16)
            vbs.append(jnp.concatenate([vn_ref[:, cs], zpad], axis=0).astype(BF16))
            s = lax.dot_general(qbd_ref[kv], kb, _NT, preferred_element_type=F32)
            ss.append(jnp.where(valid, s, NEG))
        update(ss, vbs)
        on = acc_ref[...] * (1.0 / l_ref[...])
        for kv in range(KV_B):
            o = on[kv * nr:kv * nr + rows] - lam * on[kv * nr + rows:(kv + 1) * nr]
            o_ref[kv] = _subln(o, sub_ref[...], lam_init)


def _diff_sample(pt_flat, qbd, kn8, vn8, cache_k, cache_v, lam_vecs, subln, lam_init,
                 dec_batch, dec_seq, n_pages):
    rows = dec_seq * G_B
    n_chunks = n_pages // NP
    wkv = KV_B * VD_B
    vec = pl.BlockSpec((1, HEAD_DIM), lambda b, c, pt: (0, 0))

    def page_spec(i):
        return pl.BlockSpec((None, PAGE * KV_B, VD_B),
                            lambda b, c, pt: (pt[b * n_pages + c * NP + i], 0, 0))

    grid_spec = pltpu.PrefetchScalarGridSpec(
        num_scalar_prefetch=1,
        grid=(dec_batch, n_chunks),
        in_specs=[
            vec, vec, vec, vec,
            pl.BlockSpec((1, VD_B), lambda b, c, pt: (0, 0)),
            pl.BlockSpec((None, KV_B, 2 * rows, LANES), lambda b, c, pt: (b, 0, 0, 0)),
            pl.BlockSpec((None, 8, wkv), lambda b, c, pt: (b, 0, 0)),
            pl.BlockSpec((None, 8, wkv), lambda b, c, pt: (b, 0, 0)),
        ] + [page_spec(i) for i in range(NP)] + [page_spec(i) for i in range(NP)],
        out_specs=pl.BlockSpec((None, KV_B, rows, VD_B), lambda b, c, pt: (b, 0, 0, 0)),
        scratch_shapes=[
            pltpu.VMEM((KV_B * 2 * rows, LANES), F32),
            pltpu.VMEM((KV_B * 2 * rows, LANES), F32),
            pltpu.VMEM((KV_B * 2 * rows, VD_B), F32),
        ],
    )
    return pl.pallas_call(
        functools.partial(_diff_sample_body, lam_init=lam_init, dec_seq=dec_seq,
                          n_chunks=n_chunks),
        grid_spec=grid_spec,
        out_shape=jax.ShapeDtypeStruct((dec_batch, KV_B, rows, VD_B), F32),
        compiler_params=_cparams(2),
        name="diff_sample",
    )(pt_flat, *lam_vecs, subln.reshape(1, VD_B), qbd, kn8, vn8,
      *([cache_k] * NP), *([cache_v] * NP))


def _gate_body(o_ref, z_ref, g_ref):
    g_ref[...] = (o_ref[...] * _silu(z_ref[...])).astype(BF16)


def _gate_sample(o_s, h, row_blk):
    n = o_s.shape[0]
    return pl.pallas_call(
        _gate_body,
        grid=(W_Z // 1024,),
        in_specs=[
            pl.BlockSpec((n, 1024), lambda c: (0, c)),
            pl.BlockSpec((n, 1024), lambda c: (row_blk, COL_Z // 1024 + c)),
        ],
        out_specs=pl.BlockSpec((n, 1024), lambda c: (0, c)),
        out_shape=jax.ShapeDtypeStruct((n, W_Z), BF16),
        compiler_params=_cparams(1),
        name="gate_sample",
    )(o_s, h)


def _outproj_body(gp_ref, gs_ref, w_ref, xp_ref, xs_ref, yp_ref, ys_ref, wb_ref,
                  *, n_prompt_tiles):
    i = pl.program_id(1)

    @pl.when(i == 0)
    def _():
        wb_ref[...] = w_ref[...].astype(BF16)

    @pl.when(i < n_prompt_tiles)
    def _():
        yp_ref[...] = xp_ref[...] + jnp.dot(gp_ref[...], wb_ref[...], preferred_element_type=F32)

    @pl.when(i >= n_prompt_tiles)
    def _():
        ys_ref[...] = xs_ref[...] + jnp.dot(gs_ref[...], wb_ref[...], preferred_element_type=F32)


def _outproj(gp, gs, w, xp, xs):
    seq, n_dec = gp.shape[0], gs.shape[0]
    assert n_dec == TM and seq % TM == 0
    n_prompt_tiles = seq // TM
    last = n_prompt_tiles - 1
    return pl.pallas_call(
        functools.partial(_outproj_body, n_prompt_tiles=n_prompt_tiles),
        grid=(D_MODEL // TN_OUT, n_prompt_tiles + 1),
        in_specs=[
            pl.BlockSpec((TM, W_Z), lambda j, i: (jnp.minimum(i, last), 0)),
            pl.BlockSpec((TM, W_Z), lambda j, i: (0, 0)),
            pl.BlockSpec((W_Z, TN_OUT), lambda j, i: (0, j)),
            pl.BlockSpec((TM, TN_OUT), lambda j, i: (jnp.minimum(i, last), j)),
            pl.BlockSpec((TM, TN_OUT), lambda j, i: (0, j)),
        ],
        out_specs=[
            pl.BlockSpec((TM, TN_OUT), lambda j, i: (jnp.minimum(i, last), j)),
            pl.BlockSpec((TM, TN_OUT), lambda j, i: (0, j)),
        ],
        out_shape=[
            jax.ShapeDtypeStruct((seq, D_MODEL), F32),
            jax.ShapeDtypeStruct((n_dec, D_MODEL), F32),
        ],
        scratch_shapes=[pltpu.VMEM((W_Z, TN_OUT), BF16)],
        compiler_params=_cparams(2),
        name="outproj",
    )(gp, gs, w, xp, xs)


def _rope_tables(seq, dec_batch, dec_seq, past_len):
    half = HEAD_DIM // 2
    inv_freq = 1.0 / (ROPE_THETA ** (jnp.arange(0, HEAD_DIM, 2, dtype=F32) / HEAD_DIM))
    pos = jnp.concatenate([
        jnp.arange(seq, dtype=jnp.int32),
        jnp.tile(past_len + jnp.arange(dec_seq, dtype=jnp.int32), dec_batch),
    ])
    ang = pos.astype(F32)[:, None] * inv_freq[None, :]
    cos = jnp.cos(ang)
    sin = jnp.sin(ang)
    reps = LANES // HEAD_DIM
    cos_t = jnp.concatenate([cos, cos] * reps, axis=1)
    sin_t = jnp.concatenate([-sin, sin] * reps, axis=1)
    assert cos_t.shape[1] == LANES and half * 2 == HEAD_DIM
    return cos_t, sin_t


def kernel(x_prompt, x_sample, cache_win_k, cache_win_v, cache_k, cache_v, page_table,
           norm_a, w_in_a, q_norm_a, k_norm_a, sinks_a, w_out_a,
           norm_b, w_in_b, q_norm_b, k_norm_b, lambda_q1, lambda_k1, lambda_q2, lambda_k2,
           subln_b, w_out_b):
    batch, seq, _ = x_prompt.shape
    dec_batch, dec_seq, _ = x_sample.shape
    n_pool = cache_k.shape[1]
    n_pages = page_table.shape[1]
    past_len = n_pages * cache_k.shape[2]
    n_dec = dec_batch * dec_seq
    n_rows = seq + n_dec
    assert batch == 1 and n_dec == TM and seq % TM == 0 and cache_k.shape[2] == PAGE
    assert cache_win_k.shape[2] == WINDOW and n_pages % NP == 0
    s_blk = seq // TM

    cos_t, sin_t = _rope_tables(seq, dec_batch, dec_seq, past_len)
    xp = x_prompt.reshape(seq, D_MODEL)
    xs = x_sample.reshape(n_dec, D_MODEL)

    h = _inproj(xp, xs, norm_a[0], w_in_a[0].astype(BF16))
    q, k = _qkpost(h, cos_t, sin_t, q_norm_a[0], k_norm_a[0], seq, n_rows, paged=False)
    v = h[:, COL_V:COL_V + W_V]

    gp = _win_prompt(q, k, h, sinks_a[0], seq)

    rows_a = dec_seq * G_A
    qs = q[seq:].reshape(dec_batch, dec_seq, KV_A, G_A, HEAD_DIM)
    qs = qs.transpose(0, 2, 1, 3, 4).reshape(dec_batch, KV_A, rows_a, HEAD_DIM)
    sink_tab = jnp.broadcast_to(sinks_a[0].astype(F32).reshape(KV_A, 1, G_A, 1),
                                (KV_A, dec_seq, G_A, LANES)).reshape(KV_A * rows_a, LANES)
    wkv_a = KV_A * HEAD_DIM
    kn = k[seq:].reshape(dec_batch, dec_seq, wkv_a)
    vn = v[seq:].reshape(dec_batch, dec_seq, wkv_a)
    o_s, win_k_s, win_v_s = _win_sample(
        qs, sink_tab,
        cache_win_k[0].transpose(0, 2, 3, 1).reshape(dec_batch, wkv_a, WINDOW),
        cache_win_v[0].transpose(0, 2, 3, 1).reshape(dec_batch, wkv_a, WINDOW),
        kn, vn, dec_batch, dec_seq)
    o_s = o_s.reshape(dec_batch, KV_A, dec_seq, G_A * HEAD_DIM)
    o_s = o_s.transpose(0, 2, 1, 3).reshape(n_dec, W_Q)
    gs = _gate_sample(o_s, h, s_blk)

    x1p, x1s = _outproj(gp, gs, w_out_a[0], xp, xs)

    wl = min(WINDOW, seq)
    win_k_p = k[seq - wl:seq].reshape(1, batch, wl, KV_A, HEAD_DIM)
    win_v_p = v[seq - wl:seq].reshape(1, batch, wl, KV_A, HEAD_DIM)
    win_k_s = win_k_s.reshape(dec_batch, KV_A, HEAD_DIM, WINDOW).transpose(0, 3, 1, 2)[None]
    win_v_s = win_v_s.reshape(dec_batch, KV_A, HEAD_DIM, WINDOW).transpose(0, 3, 1, 2)[None]

    lam_init = 0.8 - 0.6 * math.exp(-0.3 * 1)
    lam_vecs = [a[0].astype(F32).reshape(1, HEAD_DIM)
                for a in (lambda_q1, lambda_k1, lambda_q2, lambda_k2)]

    h = _inproj(x1p, x1s, norm_b[0], w_in_b[0].astype(BF16))
    q, k, kp_pg, vp_pg, ks_pg, vs_pg = _qkpost(h, cos_t, sin_t, q_norm_b[0], k_norm_b[0],
                                               seq, n_rows, paged=True)
    v = h[:, COL_V:COL_V + W_V]

    gp = _diff_prompt(q, k, h, lam_vecs, subln_b[0].astype(F32), lam_init, seq)

    rows_b = dec_seq * G_B
    qs = q[seq:].reshape(dec_batch, dec_seq, KV_B, G_B, 2, HEAD_DIM)
    qs = qs.transpose(0, 2, 4, 1, 3, 5).reshape(dec_batch, KV_B, 2, rows_b, HEAD_DIM)
    zq = jnp.zeros_like(qs[:, :, 0])
    qbd = jnp.concatenate([
        jnp.concatenate([qs[:, :, 0], zq], axis=-1),
        jnp.concatenate([zq, qs[:, :, 1]], axis=-1),
    ], axis=2)
    wkv_b = KV_B * VD_B
    kn = k[seq:].reshape(dec_batch, dec_seq, wkv_b)
    vn = v[seq:].reshape(dec_batch, dec_seq, wkv_b)
    padn = jnp.zeros((dec_batch, 8 - dec_seq, wkv_b), F32)
    kn8 = jnp.concatenate([kn, padn], axis=1)
    vn8 = jnp.concatenate([vn, padn], axis=1)
    o_s = _diff_sample(
        page_table.reshape(-1).astype(jnp.int32), qbd, kn8, vn8,
        cache_k[0].reshape(n_pool, PAGE * KV_B, VD_B), cache_v[0].reshape(n_pool, PAGE * KV_B, VD_B),
        lam_vecs, subln_b[0].astype(F32), lam_init, dec_batch, dec_seq, n_pages)
    o_s = o_s.reshape(dec_batch, KV_B, dec_seq, G_B * VD_B)
    o_s = o_s.transpose(0, 2, 1, 3).reshape(n_dec, W_Q)
    gs = _gate_sample(o_s, h, s_blk)

    x2p, x2s = _outproj(gp, gs, w_out_b[0], x1p, x1s)

    y_prompt = x2p.reshape(batch, seq, D_MODEL)
    y_sample = x2s.reshape(dec_batch, dec_seq, D_MODEL)
    kv_k_p = kp_pg.reshape(1, batch, seq // PAGE, PAGE, KV_B, 2 * HEAD_DIM)
    kv_v_p = vp_pg.reshape(1, batch, seq // PAGE, PAGE, KV_B, VD_B)
    kv_k_s = ks_pg.reshape(1, dec_batch, dec_seq, KV_B, 2 * HEAD_DIM)
    kv_v_s = vs_pg.reshape(1, dec_batch, dec_seq, KV_B, VD_B)

    return (y_prompt, y_sample, win_k_p, win_v_p, win_k_s, win_v_s,
            kv_k_p, kv_v_p, kv_k_s, kv_v_s)
```
